```python
import math
import jax, jax.numpy as jnp
from jax import lax
import numpy as np

D_MODEL = 1024
BATCH = 8
SEQ = 2048
DEPTH = 4

N_META = 16
N_A_LAYERS = DEPTH // 2
N_B_LAYERS = DEPTH - N_A_LAYERS
ALPHA = (2.0 * DEPTH) ** 0.25
BETA = (8.0 * DEPTH) ** -0.25
LN_EPS = 1e-5
SSM_EXPAND = 2
D_INNER = SSM_EXPAND * D_MODEL
SSM_HEAD_DIM = 64
SSM_HEADS = D_INNER // SSM_HEAD_DIM
SSM_GROUPS = 4
SSM_HPG = SSM_HEADS // SSM_GROUPS
D_STATE = 128
CONV_WIDTH = 4
SSD_CHUNK = 128
CONV_DIM = D_INNER + 2 * SSM_GROUPS * D_STATE
IN_DIM = D_INNER + CONV_DIM + SSM_HEADS
SB_HEADS = 16
SB_HEAD_DIM = 64
SB_WIDTH = SB_HEADS * SB_HEAD_DIM
SB_BLOCK = 128
SB_SCALE = SB_HEAD_DIM ** -0.5
N_EXPERTS = 16
N_EXPERT_GROUPS = 4
EXPERTS_PER_GROUP = N_EXPERTS // N_EXPERT_GROUPS
TOP_K = 2
D_EXPERT = 512

kernel_name = 'yoco_ssd_stickbreak_moe_trunk'


def layer_norm(x, g, b):
    xf = x.astype(jnp.float32)
    mu = jnp.mean(xf, -1, keepdims=True)
    var = jnp.mean(jnp.square(xf - mu), -1, keepdims=True)
    y = (xf - mu) * lax.rsqrt(var + LN_EPS)
    return (y * g.astype(jnp.float32) + b.astype(jnp.float32)).astype(x.dtype)


def causal_depthwise_conv(x, w, bias):
    y = lax.conv_general_dilated(x, w[:, None, :].astype(x.dtype), window_strides=(1,),
                                 padding=[(CONV_WIDTH - 1, 0)],
                                 dimension_numbers=('NWC', 'WIO', 'NWC'),
                                 feature_group_count=x.shape[-1])
    return y + bias.astype(x.dtype)


def ssd_chunked(x, a, bm, cm):
    bsz, L = x.shape[:2]
    pad = (-L) % SSD_CHUNK
    x = jnp.pad(x, ((0, 0), (pad, 0), (0, 0), (0, 0)))
    a = jnp.pad(a, ((0, 0), (pad, 0), (0, 0)))
    bm = jnp.pad(bm, ((0, 0), (pad, 0), (0, 0), (0, 0)))
    cm = jnp.pad(cm, ((0, 0), (pad, 0), (0, 0), (0, 0)))
    nc = (L + pad) // SSD_CHUNK
    x = x.reshape(bsz, nc, SSD_CHUNK, SSM_GROUPS, SSM_HPG, SSM_HEAD_DIM)
    a = a.reshape(bsz, nc, SSD_CHUNK, SSM_GROUPS, SSM_HPG).transpose(0, 3, 4, 1, 2)
    bm = bm.reshape(bsz, nc, SSD_CHUNK, SSM_GROUPS, D_STATE)
    cm = cm.reshape(bsz, nc, SSD_CHUNK, SSM_GROUPS, D_STATE)
    a_cum = jnp.cumsum(a, axis=-1)
    causal = jnp.tril(jnp.ones((SSD_CHUNK, SSD_CHUNK), dtype=bool))
    seg = jnp.where(causal, a_cum[..., :, None] - a_cum[..., None, :], -jnp.inf)
    decay_ls = jnp.exp(seg)
    cb = jnp.einsum('bclgn,bcsgn->bgcls', cm, bm)
    y_diag = jnp.einsum('bgcls,bgrcls,bcsgrp->bclgrp', cb, decay_ls, x)
    decay_to_end = jnp.exp(a_cum[..., -1:] - a_cum)
    chunk_states = jnp.einsum('bclgn,bgrcl,bclgrp->cbgrpn', bm, decay_to_end, x)
    chunk_decay = jnp.exp(a_cum[..., -1]).transpose(3, 0, 1, 2)

    def carry_state(state, inp):
        st, dec = inp
        return state * dec[..., None, None] + st, state

    _, states_in = lax.scan(carry_state, jnp.zeros(chunk_states.shape[1:], jnp.float32),
                            (chunk_states, chunk_decay))
    y_off = jnp.einsum('bclgn,cbgrpn,bgrcl->bclgrp', cm, states_in, jnp.exp(a_cum))
    y = (y_diag + y_off).reshape(bsz, nc * SSD_CHUNK, SSM_HEADS, SSM_HEAD_DIM)
    return y[:, pad:]


def mamba2_mixer(h, w_in, conv_w, conv_b, dt_bias, a_log, d_skip, norm_w, w_out):
    bsz, L, _ = h.shape
    f32 = jnp.float32
    proj = h @ w_in
    z = proj[..., :D_INNER]
    xbc = jax.nn.silu(causal_depthwise_conv(proj[..., D_INNER:D_INNER + CONV_DIM], conv_w, conv_b))
    dt = jax.nn.softplus(proj[..., D_INNER + CONV_DIM:].astype(f32) + dt_bias.astype(f32))
    gn = SSM_GROUPS * D_STATE
    xs = xbc[..., :D_INNER].reshape(bsz, L, SSM_HEADS, SSM_HEAD_DIM).astype(f32)
    bm = xbc[..., D_INNER:D_INNER + gn].reshape(bsz, L, SSM_GROUPS, D_STATE).astype(f32)
    cm = xbc[..., D_INNER + gn:].reshape(bsz, L, SSM_GROUPS, D_STATE).astype(f32)
    a_neg = -jnp.exp(a_log.astype(f32))
    y = ssd_chunked(xs * dt[..., None], dt * a_neg, bm, cm)
    y = y + d_skip.astype(f32)[:, None] * xs
    y = y.reshape(bsz, L, D_INNER) * jax.nn.silu(z.astype(f32))
    yg = y.reshape(bsz, L, SSM_GROUPS, D_INNER // SSM_GROUPS)
    yg = yg * lax.rsqrt(jnp.mean(jnp.square(yg), -1, keepdims=True) + LN_EPS)
    y = yg.reshape(bsz, L, D_INNER) * norm_w.astype(f32)
    return y.astype(h.dtype) @ w_out


def shared_kv(h, w_k, w_v):
    bsz, L, _ = h.shape
    pad = (-L) % SB_BLOCK

    def heads(t):
        t = t.reshape(bsz, L, SB_HEADS, SB_HEAD_DIM).transpose(0, 2, 1, 3)
        return jnp.pad(t, ((0, 0), (0, 0), (pad, 0), (0, 0)))

    return heads(h @ w_k), heads(h @ w_v)


def stick_breaking_attention(h, w_q, w_o, k, v):
    bsz, L, _ = h.shape
    pad = (-L) % SB_BLOCK
    lp = L + pad
    nb = lp // SB_BLOCK
    q = (h @ w_q).reshape(bsz, L, SB_HEADS, SB_HEAD_DIM)
    q = jnp.pad(q, ((0, 0), (pad, 0), (0, 0), (0, 0)))
    q = q.reshape(bsz, nb, SB_BLOCK, SB_HEADS, SB_HEAD_DIM).transpose(1, 0, 3, 2, 4)
    kpos = jnp.arange(lp)

    def query_block(args):
        qb, blk = args
        z = jnp.einsum('bhqd,bhkd->bhqk', qb, k).astype(jnp.float32) * SB_SCALE
        qpos = blk * SB_BLOCK + jnp.arange(SB_BLOCK)
        mask = (kpos[None, :] < qpos[:, None]) & (kpos[None, :] >= pad)
        log_beta = jax.nn.log_sigmoid(z)
        log_keep = jnp.where(mask, jax.nn.log_sigmoid(-z), 0.0)
        later = lax.cumsum(log_keep, axis=3, reverse=True) - log_keep
        w = jnp.where(mask, jnp.exp(log_beta + later), 0.0)
        return jnp.einsum('bhqk,bhkd->bhqd', w.astype(v.dtype), v)

    o = lax.map(query_block, (q, jnp.arange(nb)))
    o = o.transpose(1, 0, 3, 2, 4).reshape(bsz, lp, SB_WIDTH)[:, pad:]
    return o @ w_o


def routed_moe(h, router_w, router_b, w_gate, w_up, w_down):
    bsz, L, d = h.shape
    xt = h.reshape(-1, d)
    logits = (xt @ router_w).astype(jnp.float32) + router_b.astype(jnp.float32)
    probs = jax.nn.softmax(logits, axis=-1)
    pg = probs.reshape(-1, N_EXPERT_GROUPS, EXPERTS_PER_GROUP)
    group_score = lax.top_k(pg, TOP_K)[0].sum(-1)
    gsel = jnp.argmax(group_score, axis=-1)
    p_in = jnp.take_along_axis(pg, gsel[:, None, None], axis=1)[:, 0]
    vals, idx = lax.top_k(p_in, TOP_K)
    eid = gsel[:, None] * EXPERTS_PER_GROUP + idx
    gates = vals / jnp.sum(vals, -1, keepdims=True)
    dense_gate = jnp.sum(jax.nn.one_hot(eid, N_EXPERTS, dtype=jnp.float32) * gates[..., None], axis=1)
    hid = jax.nn.silu(jnp.einsum('nd,edf->nef', xt, w_gate)) * jnp.einsum('nd,edf->nef', xt, w_up)
    hid = hid * dense_gate[..., None].astype(hid.dtype)
    out = jnp.einsum('nef,efd->nd', hid, w_down)
    return out.reshape(bsz, L, d)


def setup_inputs(seed: int = 0) -> dict:
    key = jax.random.key(seed)
    ks = iter(jax.random.split(key, 32))
    f32 = jnp.float32

    def nrm(shape, scale):
        return scale * jax.random.normal(next(ks), shape, f32)

    x = nrm((BATCH, SEQ, D_MODEL), 1.0)
    meta_tokens = nrm((N_META, D_MODEL), 1.0)
    mamba_w_in = nrm((N_A_LAYERS, D_MODEL, IN_DIM), D_MODEL ** -0.5)
    mamba_conv_w = nrm((N_A_LAYERS, CONV_WIDTH, CONV_DIM), CONV_WIDTH ** -0.5)
    mamba_conv_b = nrm((N_A_LAYERS, CONV_DIM), 0.02)
    dt0 = jnp.exp(jax.random.uniform(next(ks), (N_A_LAYERS, SSM_HEADS), f32, math.log(1e-3), math.log(1e-1)))
    mamba_dt_bias = dt0 + jnp.log(-jnp.expm1(-dt0))
    mamba_a_log = jnp.log(jax.random.uniform(next(ks), (N_A_LAYERS, SSM_HEADS), f32, 1.0, 16.0))
    mamba_d_skip = 1.0 + nrm((N_A_LAYERS, SSM_HEADS), 0.1)
    mamba_norm_w = 1.0 + nrm((N_A_LAYERS, D_INNER), 0.02)
    mamba_w_out = nrm((N_A_LAYERS, D_INNER, D_MODEL), BETA * D_INNER ** -0.5)
    sb_w_q = nrm((N_B_LAYERS, D_MODEL, SB_WIDTH), D_MODEL ** -0.5)
    sb_w_o = nrm((N_B_LAYERS, SB_WIDTH, D_MODEL), BETA * SB_WIDTH ** -0.5)
    shared_w_k = nrm((D_MODEL, SB_WIDTH), D_MODEL ** -0.5)
    shared_w_v = nrm((D_MODEL, SB_WIDTH), BETA * D_MODEL ** -0.5)
    ln_mix_g = 1.0 + nrm((DEPTH, D_MODEL), 0.02)
    ln_mix_b = nrm((DEPTH, D_MODEL), 0.02)
    ln_ffn_g = 1.0 + nrm((DEPTH, D_MODEL), 0.02)
    ln_ffn_b = nrm((DEPTH, D_MODEL), 0.02)
    router_w = nrm((D_MODEL, N_EXPERTS), D_MODEL ** -0.5)
    router_b = nrm((N_EXPERTS,), 0.01)
    moe_w_gate = nrm((DEPTH, N_EXPERTS, D_MODEL, D_EXPERT), D_MODEL ** -0.5)
    moe_w_up = nrm((DEPTH, N_EXPERTS, D_MODEL, D_EXPERT), D_MODEL ** -0.5)
    moe_w_down = nrm((DEPTH, N_EXPERTS, D_EXPERT, D_MODEL), BETA * D_EXPERT ** -0.5)
    return {'x': x, 'meta_tokens': meta_tokens, 'mamba_w_in': mamba_w_in,
            'mamba_conv_w': mamba_conv_w, 'mamba_conv_b': mamba_conv_b,
            'mamba_dt_bias': mamba_dt_bias, 'mamba_a_log': mamba_a_log,
            'mamba_d_skip': mamba_d_skip, 'mamba_norm_w': mamba_norm_w,
            'mamba_w_out': mamba_w_out, 'sb_w_q': sb_w_q, 'sb_w_o': sb_w_o,
            'shared_w_k': shared_w_k, 'shared_w_v': shared_w_v,
            'ln_mix_g': ln_mix_g, 'ln_mix_b': ln_mix_b, 'ln_ffn_g': ln_ffn_g,
            'ln_ffn_b': ln_ffn_b, 'router_w': router_w, 'router_b': router_b,
            'moe_w_gate': moe_w_gate, 'moe_w_up': moe_w_up, 'moe_w_down': moe_w_down}


def reference(x, meta_tokens, mamba_w_in, mamba_conv_w, mamba_conv_b, mamba_dt_bias,
              mamba_a_log, mamba_d_skip, mamba_norm_w, mamba_w_out, sb_w_q, sb_w_o,
              shared_w_k, shared_w_v, ln_mix_g, ln_mix_b, ln_ffn_g, ln_ffn_b,
              router_w, router_b, moe_w_gate, moe_w_up, moe_w_down):
    bsz = x.shape[0]
    meta = jnp.broadcast_to(meta_tokens.astype(x.dtype)[None], (bsz, N_META, D_MODEL))
    h = jnp.concatenate([meta, x], axis=1)
    for layer in range(DEPTH):
        if layer == N_A_LAYERS:
            k_sh, v_sh = shared_kv(h, shared_w_k, shared_w_v)
        if layer < N_A_LAYERS:
            i = layer
            mix = mamba2_mixer(h, mamba_w_in[i], mamba_conv_w[i], mamba_conv_b[i], mamba_dt_bias[i],
                               mamba_a_log[i], mamba_d_skip[i], mamba_norm_w[i], mamba_w_out[i])
        else:
            j = layer - N_A_LAYERS
            mix = stick_breaking_attention(h, sb_w_q[j], sb_w_o[j], k_sh, v_sh)
        h = layer_norm(ALPHA * h + mix, ln_mix_g[layer], ln_mix_b[layer])
        ffn = routed_moe(h, router_w, router_b, moe_w_gate[layer], moe_w_up[layer], moe_w_down[layer])
        h = layer_norm(ALPHA * h + ffn, ln_ffn_g[layer], ln_ffn_b[layer])
    return h[:, N_META:]
```

```python
import functools
import math

import jax
import jax.numpy as jnp
from jax import lax
from jax.experimental import pallas as pl
from jax.experimental.pallas import tpu as pltpu

D_MODEL = 1024
BATCH = 8
SEQ = 2048
DEPTH = 4
N_META = 16
N_A_LAYERS = DEPTH // 2
ALPHA = (2.0 * DEPTH) ** 0.25
LN_EPS = 1e-5
D_INNER = 2048
SSM_HEAD_DIM = 64
SSM_HEADS = 32
SSM_GROUPS = 4
D_STATE = 128
CONV_WIDTH = 4
CHUNK = 128
CONV_DIM = D_INNER + 2 * SSM_GROUPS * D_STATE
SB_HEADS = 16
SB_HEAD_DIM = 64
SB_WIDTH = SB_HEADS * SB_HEAD_DIM
SB_SCALE = SB_HEAD_DIM ** -0.5
N_EXPERTS = 16
N_EXPERT_GROUPS = 4
EXPERTS_PER_GROUP = 4
D_EXPERT = 512

L_REAL = N_META + SEQ
PAD = (-L_REAL) % CHUNK
LP = L_REAL + PAD
N_CHUNKS = LP // CHUNK
NP = BATCH * LP
LANES = 128
N_PAIR_CLASSES = 6
N_CLASSES = N_EXPERT_GROUPS * N_PAIR_CLASSES
MOE_TM = 256
MOE_TILES = NP // MOE_TM + N_CLASSES
VMEM_LIMIT = 48 * 1024 * 1024

_F32 = jnp.float32
_BF16 = jnp.bfloat16


def _params(*sem):
    return pltpu.CompilerParams(dimension_semantics=sem, vmem_limit_bytes=VMEM_LIMIT)


def _sigmoid(x):
    return 1.0 / (1.0 + jnp.exp(-x))


def _softplus(x):
    return jnp.maximum(x, 0.0) + jnp.log(1.0 + jnp.exp(-jnp.abs(x)))


def _layer_norm(t, g, b):
    mu = jnp.mean(t, axis=-1, keepdims=True)
    d = t - mu
    var = jnp.mean(d * d, axis=-1, keepdims=True)
    return d * lax.rsqrt(var + LN_EPS) * g + b


def _mm_kernel(x_ref, w_ref, o_ref):
    x = x_ref[...].astype(_BF16)
    o_ref[...] = jnp.dot(x, w_ref[...], preferred_element_type=_F32).astype(o_ref.dtype)


def _matmul(x, w, n_out, tm, tn, out_dtype):
    m, k = x.shape
    return pl.pallas_call(
        _mm_kernel,
        grid=(n_out // tn, m // tm),
        in_specs=[pl.BlockSpec((tm, k), lambda j, i: (i, 0)),
                  pl.BlockSpec((k, tn), lambda j, i: (0, j))],
        out_specs=pl.BlockSpec((tm, tn), lambda j, i: (i, j)),
        out_shape=jax.ShapeDtypeStruct((m, n_out), out_dtype),
        compiler_params=_params("arbitrary", "arbitrary"),
        name="matmul",
    )(x, w)


def _add_ln_kernel(h_ref, m_ref, g_ref, b_ref, o_ref):
    t = ALPHA * h_ref[...] + m_ref[...]
    o_ref[...] = _layer_norm(t, g_ref[...], b_ref[...])


def _add_ln(h, mix, g, b, tm=512):
    n, d = h.shape
    row = pl.BlockSpec((tm, d), lambda i: (i, 0))
    vec = pl.BlockSpec((1, d), lambda i: (0, 0))
    return pl.pallas_call(
        _add_ln_kernel,
        grid=(n // tm,),
        in_specs=[row, row, vec, vec],
        out_specs=row,
        out_shape=jax.ShapeDtypeStruct((n, d), _F32),
        compiler_params=_params("arbitrary"),
        name="add_ln",
    )(h, mix, g.reshape(1, d), b.reshape(1, d))


CONV_TN = 512
CONV_HALO = 16


def _conv_kernel(x_ref, w_ref, b_ref, o_ref):
    w = w_ref[...]
    bias = b_ref[...]
    for c in range(N_CHUNKS):
        s = c * CHUNK
        if c == 0:
            cur = x_ref[0, 0:CHUNK, :].astype(_F32)
            rows = lax.broadcasted_iota(jnp.int32, cur.shape, 0)
            cur = jnp.where(rows >= PAD, cur, 0.0)
            xx = jnp.concatenate([jnp.zeros((CONV_HALO, CONV_TN), _F32), cur], axis=0)
        else:
            xx = x_ref[0, s - CONV_HALO:s + CHUNK, :].astype(_F32)
            if s - CONV_HALO < PAD:
                rows = lax.broadcasted_iota(jnp.int32, xx.shape, 0) + (s - CONV_HALO)
                xx = jnp.where(rows >= PAD, xx, 0.0)
        acc = bias + w[CONV_WIDTH - 1:CONV_WIDTH, :] * xx[CONV_HALO:, :]
        for back in range(1, CONV_WIDTH):
            shifted = pltpu.roll(xx, back, 0)
            k = CONV_WIDTH - 1 - back
            acc = acc + w[k:k + 1, :] * shifted[CONV_HALO:, :]
        o_ref[0, s:s + CHUNK, :] = (acc * _sigmoid(acc)).astype(o_ref.dtype)


def _conv_silu(proj, conv_w, conv_b):
    first = D_INNER // CONV_TN
    return pl.pallas_call(
        _conv_kernel,
        grid=(BATCH, CONV_DIM // CONV_TN),
        in_specs=[pl.BlockSpec((1, LP, CONV_TN), lambda b, j: (b, 0, first + j)),
                  pl.BlockSpec((CONV_WIDTH, CONV_TN), lambda b, j: (0, j)),
                  pl.BlockSpec((1, CONV_TN), lambda b, j: (0, j))],
        out_specs=pl.BlockSpec((1, LP, CONV_TN), lambda b, j: (b, 0, j)),
        out_shape=jax.ShapeDtypeStruct((BATCH, LP, CONV_DIM), _BF16),
        compiler_params=_params("arbitrary", "arbitrary"),
        name="conv_silu",
    )(proj, conv_w, conv_b.reshape(1, CONV_DIM))


N_HEAD_PAIRS = SSM_HEADS // 2
PAIRS_PER_GROUP = N_HEAD_PAIRS // SSM_GROUPS


def _ssd_kernel(xs_ref, b_ref, c_ref, dtr_ref, dtb_ref, aneg_ref, dskip_ref, ltri_ref, exp_ref,
                y_ref, state_ref):
    c = pl.program_id(1)

    @pl.when(c == 0)
    def _():
        state_ref[...] = jnp.zeros_like(state_ref)

    rows = lax.broadcasted_iota(jnp.int32, (CHUNK, LANES), 0)
    cols = lax.broadcasted_iota(jnp.int32, (CHUNK, LANES), 1)
    causal = cols <= rows
    left = cols < SSM_HEAD_DIM

    dt = _softplus(dtr_ref[0] + dtb_ref[...])
    dt = jnp.where(rows + c * CHUNK >= PAD, dt, 0.0)
    a = dt * aneg_ref[...]
    a_cum = jnp.dot(ltri_ref[...], a, preferred_element_type=_F32,
                    precision=lax.Precision.HIGHEST)
    a_cum_t = a_cum.T
    total = a_cum[CHUNK - 1:CHUNK, :]
    from_start = jnp.exp(a_cum)
    to_end = jnp.exp(total - a_cum)
    chunk_decay = jnp.exp(jnp.broadcast_to(total, (8, LANES)))

    expand = exp_ref[...]
    dt_x = jnp.dot(dt.astype(_BF16), expand, preferred_element_type=_F32)
    to_end_x = jnp.dot(to_end.astype(_BF16), expand, preferred_element_type=_F32)
    chunk_decay_x = jnp.dot(chunk_decay.astype(_BF16), expand, preferred_element_type=_F32)

    for g in range(SSM_GROUPS):
        bg = b_ref[0, :, g * D_STATE:(g + 1) * D_STATE]
        cg = c_ref[0, :, g * D_STATE:(g + 1) * D_STATE]
        cb = lax.dot_general(cg, bg, (((1,), (1,)), ((), ())), preferred_element_type=_F32)
        bg_t = bg.astype(_F32).T.astype(_BF16)
        cg32 = cg.astype(_F32)
        for jj in range(PAIRS_PER_GROUP):
            j = g * PAIRS_PER_GROUP + jj
            sl = slice(j * LANES, (j + 1) * LANES)
            x2 = xs_ref[0, :, sl].astype(_F32)
            xdt = x2 * dt_x[:, sl]
            parts = []
            for h in (2 * j, 2 * j + 1):
                seg = a_cum[:, h:h + 1] - a_cum_t[h:h + 1, :]
                decay = jnp.where(causal, jnp.exp(seg), 0.0)
                parts.append((cb * decay).astype(_BF16))
            for h in (2 * j, 2 * j + 1):
                parts.append((cg32 * from_start[:, h:h + 1]).astype(_BF16))
            lhs = jnp.concatenate(parts, axis=1)
            state = state_ref[j]
            xb = xdt.astype(_BF16)
            sb = state.astype(_BF16)
            zero = jnp.zeros_like(xb)
            rhs = jnp.concatenate([jnp.where(left, xb, zero), jnp.where(left, zero, xb),
                                   jnp.where(left, sb, zero), jnp.where(left, zero, sb)], axis=0)
            y2 = jnp.dot(lhs, rhs, preferred_element_type=_F32)
            y2 = y2 + dskip_ref[:, sl] * x2
            y_ref[0, :, sl] = y2.astype(y_ref.dtype)
            upd = jnp.dot(bg_t, (xdt * to_end_x[:, sl]).astype(_BF16), preferred_element_type=_F32)
            state_ref[j] = state * chunk_decay_x[0:1, sl] + upd


def _ssd(xbc, dt_raw, dt_bias, a_log, d_skip):
    pad_h = LANES - SSM_HEADS
    dtb = jnp.pad(dt_bias.astype(_F32), (0, pad_h)).reshape(1, LANES)
    aneg = jnp.pad(-jnp.exp(a_log.astype(_F32)), (0, pad_h)).reshape(1, LANES)
    dskip = jnp.repeat(d_skip.astype(_F32), SSM_HEAD_DIM).reshape(1, D_INNER)
    ltri = jnp.tril(jnp.ones((CHUNK, CHUNK), _F32))
    expand = (jnp.arange(LANES)[:, None] == (jnp.arange(D_INNER)[None, :] // SSM_HEAD_DIM)).astype(_BF16)
    gn = SSM_GROUPS * D_STATE
    vec = lambda n: pl.BlockSpec((1, n), lambda b, c: (0, 0))
    return pl.pallas_call(
        _ssd_kernel,
        grid=(BATCH, N_CHUNKS),
        in_specs=[pl.BlockSpec((1, CHUNK, D_INNER), lambda b, c: (b, c, 0)),
                  pl.BlockSpec((1, CHUNK, gn), lambda b, c: (b, c, D_INNER // gn)),
                  pl.BlockSpec((1, CHUNK, gn), lambda b, c: (b, c, D_INNER // gn + 1)),
                  pl.BlockSpec((1, CHUNK, LANES), lambda b, c: (b, c, 0)),
                  vec(LANES), vec(LANES), vec(D_INNER),
                  pl.BlockSpec((CHUNK, CHUNK), lambda b, c: (0, 0)),
                  pl.BlockSpec((LANES, D_INNER), lambda b, c: (0, 0))],
        out_specs=pl.BlockSpec((1, CHUNK, D_INNER), lambda b, c: (b, c, 0)),
        out_shape=jax.ShapeDtypeStruct((BATCH, LP, D_INNER), _BF16),
        scratch_shapes=[pltpu.VMEM((N_HEAD_PAIRS, D_STATE, LANES), _F32)],
        compiler_params=_params("arbitrary", "arbitrary"),
        name="ssd",
    )(xbc, xbc, xbc, dt_raw, dtb, aneg, dskip, ltri, expand)


def _mamba_out_kernel(y_ref, z_ref, nw_ref, w_ref, h_ref, g_ref, b_ref, o_ref):
    y = y_ref[...].astype(_F32)
    z = z_ref[...].astype(_F32)
    yg = y * (z * _sigmoid(z))
    gw = D_INNER // SSM_GROUPS
    parts = []
    for g in range(SSM_GROUPS):
        blk = yg[:, g * gw:(g + 1) * gw]
        ms = jnp.mean(blk * blk, axis=-1, keepdims=True)
        parts.append(blk * lax.rsqrt(ms + LN_EPS))
    yn = jnp.concatenate(parts, axis=1) * nw_ref[...]
    mix = jnp.dot(yn.astype(_BF16), w_ref[...], preferred_element_type=_F32)
    o_ref[...] = _layer_norm(ALPHA * h_ref[...] + mix, g_ref[...], b_ref[...])


def _mamba_out(y, proj, norm_w, w_out, h, g, b, tm=256):
    n = h.shape[0]
    vec = lambda d: pl.BlockSpec((1, d), lambda i: (0, 0))
    return pl.pallas_call(
        _mamba_out_kernel,
        grid=(n // tm,),
        in_specs=[pl.BlockSpec((tm, D_INNER), lambda i: (i, 0)),
                  pl.BlockSpec((tm, D_INNER), lambda i: (i, 0)),
                  vec(D_INNER),
                  pl.BlockSpec((D_INNER, D_MODEL), lambda i: (0, 0)),
                  pl.BlockSpec((tm, D_MODEL), lambda i: (i, 0)),
                  vec(D_MODEL), vec(D_MODEL)],
        out_specs=pl.BlockSpec((tm, D_MODEL), lambda i: (i, 0)),
        out_shape=jax.ShapeDtypeStruct((n, D_MODEL), _F32),
        compiler_params=_params("arbitrary"),
        name="mamba_out",
    )(y, proj, norm_w.reshape(1, D_INNER).astype(_F32), w_out, h,
      g.reshape(1, D_MODEL), b.reshape(1, D_MODEL))


def _attn_out_kernel(o_ref_in, w_ref, h_ref, g_ref, b_ref, o_ref):
    mix = jnp.dot(o_ref_in[...], w_ref[...], preferred_element_type=_F32)
    o_ref[...] = _layer_norm(ALPHA * h_ref[...] + mix, g_ref[...], b_ref[...])


def _attn_out(o, w_o, h, g, b, tm=512):
    n = h.shape[0]
    vec = pl.BlockSpec((1, D_MODEL), lambda i: (0, 0))
    return pl.pallas_call(
        _attn_out_kernel,
        grid=(n // tm,),
        in_specs=[pl.BlockSpec((tm, SB_WIDTH), lambda i: (i, 0)),
                  pl.BlockSpec((SB_WIDTH, D_MODEL), lambda i: (0, 0)),
                  pl.BlockSpec((tm, D_MODEL), lambda i: (i, 0)),
                  vec, vec],
        out_specs=pl.BlockSpec((tm, D_MODEL), lambda i: (i, 0)),
        out_shape=jax.ShapeDtypeStruct((n, D_MODEL), _F32),
        compiler_params=_params("arbitrary"),
        name="attn_out",
    )(o, w_o, h, g.reshape(1, D_MODEL), b.reshape(1, D_MODEL))


SB_BLK = 128


def _sb_kernel(q_ref, k_ref, v_ref, tri_ref, o_ref):
    qi = pl.program_id(2)
    q = q_ref[0]
    tri = tri_ref[...]
    rows = lax.broadcasted_iota(jnp.int32, (SB_BLK, SB_BLK), 0)
    cols = lax.broadcasted_iota(jnp.int32, (SB_BLK, SB_BLK), 1)
    qpos = rows + qi * SB_BLK
    left = cols < SB_HEAD_DIM
    zero = jnp.zeros_like(q)
    acc = jnp.zeros((SB_BLK, LANES), _F32)
    for hh in range(2):
        sel = left if hh == 0 else jnp.logical_not(left)
        qm = jnp.where(sel, q, zero)

        def body(t, carry, qm=qm, sel=sel):
            later_blocks, acc = carry
            kb = qi - t
            start = pl.multiple_of(kb * SB_BLK, SB_BLK)
            kblk = k_ref[0, pl.ds(start, SB_BLK), :]
            vblk = v_ref[0, pl.ds(start, SB_BLK), :]
            vm = jnp.where(sel, vblk, zero)
            z = lax.dot_general(qm, kblk, (((1,), (1,)), ((), ())), preferred_element_type=_F32)
            soft = jnp.log(1.0 + jnp.exp(-jnp.abs(z)))
            log_beta = jnp.minimum(z, 0.0) - soft
            log_keep = log_beta - z
            kpos = cols + kb * SB_BLK
            valid = jnp.logical_and(kpos >= PAD, kpos < qpos)
            log_keep = jnp.where(valid, log_keep, 0.0)
            hi = log_keep.astype(_BF16)
            lo = (log_keep - hi.astype(_F32)).astype(_BF16)
            later = (jnp.dot(hi, tri, preferred_element_type=_F32)
                     + jnp.dot(lo, tri, preferred_element_type=_F32))
            s = log_beta + later[:, :SB_BLK] + later_blocks
            w = jnp.where(valid, jnp.exp(s), 0.0)
            acc = acc + jnp.dot(w.astype(_BF16), vm, preferred_element_type=_F32)
            return later_blocks + later[:, SB_BLK:], acc

        _, acc = lax.fori_loop(0, qi + 1, body, (jnp.zeros((SB_BLK, LANES), _F32), acc))
    o_ref[0] = acc.astype(o_ref.dtype)


def _stick_breaking(q, kv):
    j = jnp.arange(SB_BLK)
    tri = jnp.concatenate([(j[:, None] > j[None, :]).astype(_BF16),
                           jnp.ones((SB_BLK, LANES), _BF16)], axis=1)
    nq = LP // SB_BLK
    return pl.pallas_call(
        _sb_kernel,
        grid=(BATCH, SB_WIDTH // LANES, nq),
        in_specs=[pl.BlockSpec((1, SB_BLK, LANES), lambda b, p, i: (b, i, p)),
                  pl.BlockSpec((1, LP, LANES), lambda b, p, i: (b, 0, p)),
                  pl.BlockSpec((1, LP, LANES), lambda b, p, i: (b, 0, p + SB_WIDTH // LANES)),
                  pl.BlockSpec((SB_BLK, SB_BLK + LANES), lambda b, p, i: (0, 0))],
        out_specs=pl.BlockSpec((1, SB_BLK, LANES), lambda b, p, i: (b, i, p)),
        out_shape=jax.ShapeDtypeStruct((BATCH, LP, SB_WIDTH), _BF16),
        compiler_params=_params("arbitrary", "arbitrary", "arbitrary"),
        name="stick_breaking",
    )(q, kv, kv, tri)


ROUTER_TM = 512


def _router_kernel(h_ref, w_ref, b_ref, cls_ref, glo_ref, ghi_ref):
    logits = lax.dot_general(w_ref[...], h_ref[...], (((1,), (1,)), ((), ())),
                             preferred_element_type=_F32,
                             precision=lax.Precision.HIGHEST) + b_ref[...]
    m = jnp.max(logits, axis=0, keepdims=True)
    e = jnp.exp(logits - m)
    p = e / jnp.sum(e, axis=0, keepdims=True)
    ng = N_EXPERT_GROUPS
    pj = [p[j * ng:(j + 1) * ng, :] for j in range(EXPERTS_PER_GROUP)]
    m1 = jnp.maximum(jnp.maximum(pj[0], pj[1]), jnp.maximum(pj[2], pj[3]))
    i1 = jnp.where(pj[0] == m1, 0, jnp.where(pj[1] == m1, 1, jnp.where(pj[2] == m1, 2, 3)))
    qj = [jnp.where(i1 == j, -1.0, pj[j]) for j in range(EXPERTS_PER_GROUP)]
    m2 = jnp.maximum(jnp.maximum(qj[0], qj[1]), jnp.maximum(qj[2], qj[3]))
    i2 = jnp.where(qj[0] == m2, 0, jnp.where(qj[1] == m2, 1, jnp.where(qj[2] == m2, 2, 3)))
    score = m1 + m2
    gid = lax.broadcasted_iota(jnp.int32, score.shape, 0)
    best = jnp.max(score, axis=0, keepdims=True)
    gsel = jnp.min(jnp.where(score == best, gid, ng), axis=0, keepdims=True)
    chosen = gid == gsel
    pick_f = lambda v: jnp.sum(jnp.where(chosen, v, 0.0), axis=0, keepdims=True)
    pick_i = lambda v: jnp.sum(jnp.where(chosen, v, 0), axis=0, keepdims=True)
    v1, v2, j1, j2 = pick_f(m1), pick_f(m2), pick_i(i1), pick_i(i2)
    denom = v1 + v2
    g1 = v1 / denom
    g2 = v2 / denom
    first_low = j1 < j2
    lo = jnp.where(first_low, j1, j2)
    hi = jnp.where(first_low, j2, j1)
    base = jnp.where(lo == 0, 0, jnp.where(lo == 1, 3, 5))
    cls_ref[...] = gsel * N_PAIR_CLASSES + base + hi - lo - 1
    glo_ref[...] = jnp.where(first_low, g1, g2)
    ghi_ref[...] = jnp.where(first_low, g2, g1)


def _router(h, router_w, router_b):
    n = h.shape[0]
    perm = jnp.array([g * EXPERTS_PER_GROUP + j for j in range(EXPERTS_PER_GROUP)
                      for g in range(N_EXPERT_GROUPS)])
    w_t = router_w.astype(_F32).T[perm]
    b_t = router_b.astype(_F32)[perm].reshape(N_EXPERTS, 1)
    out = pl.BlockSpec((1, ROUTER_TM), lambda i: (0, i))
    return pl.pallas_call(
        _router_kernel,
        grid=(n // ROUTER_TM,),
        in_specs=[pl.BlockSpec((ROUTER_TM, D_MODEL), lambda i: (i, 0)),
                  pl.BlockSpec((N_EXPERTS, D_MODEL), lambda i: (0, 0)),
                  pl.BlockSpec((N_EXPERTS, 1), lambda i: (0, 0))],
        out_specs=[out, out, out],
        out_shape=[jax.ShapeDtypeStruct((1, n), jnp.int32),
                   jax.ShapeDtypeStruct((1, n), _F32),
                   jax.ShapeDtypeStruct((1, n), _F32)],
        compiler_params=_params("arbitrary"),
        name="router",
    )(h, w_t, b_t)


def _class_experts():
    lo, hi = [], []
    for g in range(N_EXPERT_GROUPS):
        for a in range(EXPERTS_PER_GROUP):
            for b in range(a + 1, EXPERTS_PER_GROUP):
                lo.append(g * EXPERTS_PER_GROUP + a)
                hi.append(g * EXPERTS_PER_GROUP + b)
    return jnp.array(lo, jnp.int32), jnp.array(hi, jnp.int32)


def _expert_kernel(elo_ref, ehi_ref, nused_ref, x_ref, wg1, wu1, wd1, wg2, wu2, wd2,
                   g1_ref, g2_ref, o_ref):
    t = pl.program_id(0)

    @pl.when(t < nused_ref[0])
    def _():
        x = x_ref[...]

        def ffn(wg, wu, wd, gate_ref):
            a = jnp.dot(x, wg[0], preferred_element_type=_F32)
            u = jnp.dot(x, wu[0], preferred_element_type=_F32)
            gate = jnp.tile(gate_ref[...], (1, D_EXPERT // LANES))
            hid = (a * _sigmoid(a)) * u * gate
            return jnp.dot(hid.astype(_BF16), wd[0], preferred_element_type=_F32)

        o_ref[...] = ffn(wg1, wu1, wd1, g1_ref) + ffn(wg2, wu2, wd2, g2_ref)

    @pl.when(t >= nused_ref[0])
    def _():
        o_ref[...] = jnp.zeros_like(o_ref)


def _experts(x_sorted, g_lo, g_hi, tile_lo, tile_hi, n_used, w_gate, w_up, w_down):
    up_spec = lambda which: pl.BlockSpec(
        (1, D_MODEL, D_EXPERT), lambda t, lo, hi, nu: ((lo, hi)[which][t], 0, 0))
    down_spec = lambda which: pl.BlockSpec(
        (1, D_EXPERT, D_MODEL), lambda t, lo, hi, nu: ((lo, hi)[which][t], 0, 0))
    row = lambda d: pl.BlockSpec((MOE_TM, d), lambda t, lo, hi, nu: (t, 0))
    grid_spec = pltpu.PrefetchScalarGridSpec(
        num_scalar_prefetch=3,
        grid=(MOE_TILES,),
        in_specs=[row(D_MODEL), up_spec(0), up_spec(0), down_spec(0),
                  up_spec(1), up_spec(1), down_spec(1), row(LANES), row(LANES)],
        out_specs=row(D_MODEL),
    )
    return pl.pallas_call(
        _expert_kernel,
        grid_spec=grid_spec,
        out_shape=jax.ShapeDtypeStruct((MOE_TILES * MOE_TM, D_MODEL), _F32),
        compiler_params=_params("arbitrary"),
        name="experts",
    )(tile_lo, tile_hi, n_used, x_sorted, w_gate, w_up, w_down, w_gate, w_up, w_down, g_lo, g_hi)


def _routed_moe(h, router_w, router_b, w_gate, w_up, w_down):
    cls, glo, ghi = _router(h, router_w, router_b)
    cls, glo, ghi = cls[0], glo[0], ghi[0]
    order = jnp.argsort(cls, stable=True).astype(jnp.int32)
    sorted_cls = cls[order]
    counts = jnp.bincount(cls, length=N_CLASSES).astype(jnp.int32)
    starts = jnp.cumsum(counts) - counts
    tiles = (counts + MOE_TM - 1) // MOE_TM
    tile_end = jnp.cumsum(tiles)
    padded_start = (tile_end - tiles) * MOE_TM
    dest_sorted = padded_start[sorted_cls] + jnp.arange(NP, dtype=jnp.int32) - starts[sorted_cls]
    total_rows = MOE_TILES * MOE_TM
    src = jnp.zeros((total_rows,), jnp.int32).at[dest_sorted].set(order)
    occupied = jnp.zeros((total_rows,), _F32).at[dest_sorted].set(1.0)
    pos = jnp.zeros((NP,), jnp.int32).at[order].set(dest_sorted)
    n_used = tile_end[-1]
    tile_ids = jnp.minimum(jnp.arange(MOE_TILES, dtype=jnp.int32), n_used - 1)
    tile_cls = jnp.searchsorted(tile_end, tile_ids, side="right").astype(jnp.int32)
    class_lo, class_hi = _class_experts()
    x_sorted = h.astype(_BF16)[src]
    g_lo = jnp.broadcast_to((glo[src] * occupied)[:, None], (total_rows, LANES))
    g_hi = jnp.broadcast_to((ghi[src] * occupied)[:, None], (total_rows, LANES))
    y_sorted = _experts(x_sorted, g_lo, g_hi, class_lo[tile_cls], class_hi[tile_cls],
                        n_used.reshape(1).astype(jnp.int32), w_gate, w_up, w_down)
    return y_sorted[pos]


def kernel(x, meta_tokens, mamba_w_in, mamba_conv_w, mamba_conv_b, mamba_dt_bias, mamba_a_log,
           mamba_d_skip, mamba_norm_w, mamba_w_out, sb_w_q, sb_w_o, shared_w_k, shared_w_v,
           ln_mix_g, ln_mix_b, ln_ffn_g, ln_ffn_b, router_w, router_b, moe_w_gate, moe_w_up,
           moe_w_down):
    bsz = x.shape[0]
    meta = jnp.broadcast_to(meta_tokens.astype(x.dtype)[None], (bsz, N_META, D_MODEL))
    h = jnp.concatenate([jnp.zeros((bsz, PAD, D_MODEL), x.dtype), meta, x], axis=1)
    h = h.reshape(NP, D_MODEL)
    zx_dim = D_INNER + CONV_DIM
    kv = None
    for layer in range(DEPTH):
        if layer < N_A_LAYERS:
            w_in = mamba_w_in[layer].astype(_BF16)
            w_dt = jnp.pad(w_in[:, zx_dim:], ((0, 0), (0, LANES - SSM_HEADS)))
            proj = _matmul(h, w_in, zx_dim, 512, 1024, _BF16)
            dt_raw = _matmul(h, w_dt, LANES, 512, LANES, _F32)
            xbc = _conv_silu(proj.reshape(BATCH, LP, zx_dim), mamba_conv_w[layer].astype(_F32),
                             mamba_conv_b[layer].astype(_F32))
            y = _ssd(xbc, dt_raw.reshape(BATCH, LP, LANES), mamba_dt_bias[layer],
                     mamba_a_log[layer], mamba_d_skip[layer])
            h = _mamba_out(y.reshape(NP, D_INNER), proj, mamba_norm_w[layer],
                           mamba_w_out[layer].astype(_BF16), h, ln_mix_g[layer], ln_mix_b[layer])
        else:
            j = layer - N_A_LAYERS
            if kv is None:
                w_kv = jnp.concatenate([shared_w_k, shared_w_v], axis=1).astype(_BF16)
                kv = _matmul(h, w_kv, 2 * SB_WIDTH, 512, 1024, _BF16).reshape(BATCH, LP, 2 * SB_WIDTH)
            q = _matmul(h, (sb_w_q[j] * SB_SCALE).astype(_BF16), SB_WIDTH, 512, 1024, _BF16)
            o = _stick_breaking(q.reshape(BATCH, LP, SB_WIDTH), kv)
            h = _attn_out(o.reshape(NP, SB_WIDTH), sb_w_o[j].astype(_BF16), h,
                          ln_mix_g[layer], ln_mix_b[layer])
        ffn = _routed_moe(h, router_w, router_b, moe_w_gate[layer].astype(_BF16),
                          moe_w_up[layer].astype(_BF16), moe_w_down[layer].astype(_BF16))
        h = _add_ln(h, ffn, ln_ffn_g[layer], ln_ffn_b[layer])
    return h.reshape(bsz, LP, D_MODEL)[:, PAD + N_META:]
```

```python
import functools
import math

import jax
import jax.numpy as jnp
from jax import lax
from jax.experimental import pallas as pl
from jax.experimental.pallas import tpu as pltpu

D_MODEL = 1024
BATCH = 8
SEQ = 2048
DEPTH = 4
N_META = 16
N_A_LAYERS = DEPTH // 2
ALPHA = (2.0 * DEPTH) ** 0.25
LN_EPS = 1e-5
D_INNER = 2048
SSM_HEAD_DIM = 64
SSM_HEADS = 32
SSM_GROUPS = 4
D_STATE = 128
CONV_WIDTH = 4
CHUNK = 128
CONV_DIM = D_INNER + 2 * SSM_GROUPS * D_STATE
SB_HEADS = 16
SB_HEAD_DIM = 64
SB_WIDTH = SB_HEADS * SB_HEAD_DIM
SB_SCALE = SB_HEAD_DIM ** -0.5
N_EXPERTS = 16
N_EXPERT_GROUPS = 4
EXPERTS_PER_GROUP = 4
D_EXPERT = 512

L_REAL = N_META + SEQ
PAD = (-L_REAL) % CHUNK
LP = L_REAL + PAD
N_CHUNKS = LP // CHUNK
NP = BATCH * LP
LANES = 128
N_PAIR_CLASSES = 6
N_CLASSES = N_EXPERT_GROUPS * N_PAIR_CLASSES
MOE_TM = 256
MOE_TILES = NP // MOE_TM + N_CLASSES
VMEM_LIMIT = 48 * 1024 * 1024

_F32 = jnp.float32
_BF16 = jnp.bfloat16


def _params(*sem):
    return pltpu.CompilerParams(dimension_semantics=sem, vmem_limit_bytes=VMEM_LIMIT)


def _sigmoid(x):
    return 1.0 / (1.0 + jnp.exp(-x))


def _softplus(x):
    return jnp.maximum(x, 0.0) + jnp.log(1.0 + jnp.exp(-jnp.abs(x)))


def _layer_norm(t, g, b):
    mu = jnp.mean(t, axis=-1, keepdims=True)
    d = t - mu
    var = jnp.mean(d * d, axis=-1, keepdims=True)
    return d * lax.rsqrt(var + LN_EPS) * g + b


def _mm_kernel(x_ref, w_ref, o_ref):
    x = x_ref[...].astype(_BF16)
    o_ref[...] = jnp.dot(x, w_ref[...], preferred_element_type=_F32).astype(o_ref.dtype)


def _matmul(x, w, n_out, tm, tn, out_dtype):
    m, k = x.shape
    return pl.pallas_call(
        _mm_kernel,
        grid=(n_out // tn, m // tm),
        in_specs=[pl.BlockSpec((tm, k), lambda j, i: (i, 0)),
                  pl.BlockSpec((k, tn), lambda j, i: (0, j))],
        out_specs=pl.BlockSpec((tm, tn), lambda j, i: (i, j)),
        out_shape=jax.ShapeDtypeStruct((m, n_out), out_dtype),
        compiler_params=_params("arbitrary", "arbitrary"),
        name="matmul",
    )(x, w)


def _add_ln_kernel(h_ref, m_ref, g_ref, b_ref, o_ref):
    t = ALPHA * h_ref[...] + m_ref[...]
    o_ref[...] = _layer_norm(t, g_ref[...], b_ref[...])


def _add_ln(h, mix, g, b, tm=512):
    n, d = h.shape
    row = pl.BlockSpec((tm, d), lambda i: (i, 0))
    vec = pl.BlockSpec((1, d), lambda i: (0, 0))
    return pl.pallas_call(
        _add_ln_kernel,
        grid=(n // tm,),
        in_specs=[row, row, vec, vec],
        out_specs=row,
        out_shape=jax.ShapeDtypeStruct((n, d), _F32),
        compiler_params=_params("arbitrary"),
        name="add_ln",
    )(h, mix, g.reshape(1, d), b.reshape(1, d))


CONV_TN = 512
CONV_HALO = 16


def _conv_kernel(x_ref, w_ref, b_ref, o_ref):
    w = w_ref[...]
    bias = b_ref[...]
    for c in range(N_CHUNKS):
        s = c * CHUNK
        if c == 0:
            cur = x_ref[0, 0:CHUNK, :].astype(_F32)
            rows = lax.broadcasted_iota(jnp.int32, cur.shape, 0)
            cur = jnp.where(rows >= PAD, cur, 0.0)
            xx = jnp.concatenate([jnp.zeros((CONV_HALO, CONV_TN), _F32), cur], axis=0)
        else:
            xx = x_ref[0, s - CONV_HALO:s + CHUNK, :].astype(_F32)
            if s - CONV_HALO < PAD:
                rows = lax.broadcasted_iota(jnp.int32, xx.shape, 0) + (s - CONV_HALO)
                xx = jnp.where(rows >= PAD, xx, 0.0)
        acc = bias + w[CONV_WIDTH - 1:CONV_WIDTH, :] * xx[CONV_HALO:, :]
        for back in range(1, CONV_WIDTH):
            shifted = pltpu.roll(xx, back, 0)
            k = CONV_WIDTH - 1 - back
            acc = acc + w[k:k + 1, :] * shifted[CONV_HALO:, :]
        o_ref[0, s:s + CHUNK, :] = (acc * _sigmoid(acc)).astype(o_ref.dtype)


def _conv_silu(proj, conv_w, conv_b):
    first = D_INNER // CONV_TN
    return pl.pallas_call(
        _conv_kernel,
        grid=(BATCH, CONV_DIM // CONV_TN),
        in_specs=[pl.BlockSpec((1, LP, CONV_TN), lambda b, j: (b, 0, first + j)),
                  pl.BlockSpec((CONV_WIDTH, CONV_TN), lambda b, j: (0, j)),
                  pl.BlockSpec((1, CONV_TN), lambda b, j: (0, j))],
        out_specs=pl.BlockSpec((1, LP, CONV_TN), lambda b, j: (b, 0, j)),
        out_shape=jax.ShapeDtypeStruct((BATCH, LP, CONV_DIM), _BF16),
        compiler_params=_params("arbitrary", "arbitrary"),
        name="conv_silu",
    )(proj, conv_w, conv_b.reshape(1, CONV_DIM))


N_HEAD_PAIRS = SSM_HEADS // 2
PAIRS_PER_GROUP = N_HEAD_PAIRS // SSM_GROUPS


def _ssd_kernel(xs_ref, b_ref, c_ref, dtr_ref, dtb_ref, aneg_ref, dskip_ref, ltri_ref, exp_ref,
                y_ref, state_ref):
    c = pl.program_id(1)

    @pl.when(c == 0)
    def _():
        state_ref[...] = jnp.zeros_like(state_ref)

    rows = lax.broadcasted_iota(jnp.int32, (CHUNK, LANES), 0)
    cols = lax.broadcasted_iota(jnp.int32, (CHUNK, LANES), 1)
    causal = cols <= rows
    left = cols < SSM_HEAD_DIM

    dt = _softplus(dtr_ref[0] + dtb_ref[...])
    dt = jnp.where(rows + c * CHUNK >= PAD, dt, 0.0)
    a = dt * aneg_ref[...]
    a_cum = jnp.dot(ltri_ref[...], a, preferred_element_type=_F32,
                    precision=lax.Precision.HIGHEST)
    a_cum_t = a_cum.T
    total = a_cum[CHUNK - 1:CHUNK, :]
    from_start = jnp.exp(a_cum)
    to_end = jnp.exp(total - a_cum)
    chunk_decay = jnp.exp(jnp.broadcast_to(total, (8, LANES)))

    expand = exp_ref[...]
    dt_x = jnp.dot(dt.astype(_BF16), expand, preferred_element_type=_F32)
    to_end_x = jnp.dot(to_end.astype(_BF16), expand, preferred_element_type=_F32)
    chunk_decay_x = jnp.dot(chunk_decay.astype(_BF16), expand, preferred_element_type=_F32)

    for g in range(SSM_GROUPS):
        bg = b_ref[0, :, g * D_STATE:(g + 1) * D_STATE]
        cg = c_ref[0, :, g * D_STATE:(g + 1) * D_STATE]
        cb = lax.dot_general(cg, bg, (((1,), (1,)), ((), ())), preferred_element_type=_F32)
        bg_t = bg.astype(_F32).T.astype(_BF16)
        cg32 = cg.astype(_F32)
        for jj in range(PAIRS_PER_GROUP):
            j = g * PAIRS_PER_GROUP + jj
            sl = slice(j * LANES, (j + 1) * LANES)
            x2 = xs_ref[0, :, sl].astype(_F32)
            xdt = x2 * dt_x[:, sl]
            parts = []
            for h in (2 * j, 2 * j + 1):
                seg = a_cum[:, h:h + 1] - a_cum_t[h:h + 1, :]
                decay = jnp.where(causal, jnp.exp(seg), 0.0)
                parts.append((cb * decay).astype(_BF16))
            for h in (2 * j, 2 * j + 1):
                parts.append((cg32 * from_start[:, h:h + 1]).astype(_BF16))
            lhs = jnp.concatenate(parts, axis=1)
            state = state_ref[j]
            xb = xdt.astype(_BF16)
            sb = state.astype(_BF16)
            zero = jnp.zeros_like(xb)
            rhs = jnp.concatenate([jnp.where(left, xb, zero), jnp.where(left, zero, xb),
                                   jnp.where(left, sb, zero), jnp.where(left, zero, sb)], axis=0)
            y2 = jnp.dot(lhs, rhs, preferred_element_type=_F32)
            y2 = y2 + dskip_ref[:, sl] * x2
            y_ref[0, :, sl] = y2.astype(y_ref.dtype)
            upd = jnp.dot(bg_t, (xdt * to_end_x[:, sl]).astype(_BF16), preferred_element_type=_F32)
            state_ref[j] = state * chunk_decay_x[0:1, sl] + upd


def _ssd(xbc, dt_raw, dt_bias, a_log, d_skip):
    pad_h = LANES - SSM_HEADS
    dtb = jnp.pad(dt_bias.astype(_F32), (0, pad_h)).reshape(1, LANES)
    aneg = jnp.pad(-jnp.exp(a_log.astype(_F32)), (0, pad_h)).reshape(1, LANES)
    dskip = jnp.repeat(d_skip.astype(_F32), SSM_HEAD_DIM).reshape(1, D_INNER)
    ltri = jnp.tril(jnp.ones((CHUNK, CHUNK), _F32))
    expand = (jnp.arange(LANES)[:, None] == (jnp.arange(D_INNER)[None, :] // SSM_HEAD_DIM)).astype(_BF16)
    gn = SSM_GROUPS * D_STATE
    vec = lambda n: pl.BlockSpec((1, n), lambda b, c: (0, 0))
    return pl.pallas_call(
        _ssd_kernel,
        grid=(BATCH, N_CHUNKS),
        in_specs=[pl.BlockSpec((1, CHUNK, D_INNER), lambda b, c: (b, c, 0)),
                  pl.BlockSpec((1, CHUNK, gn), lambda b, c: (b, c, D_INNER // gn)),
                  pl.BlockSpec((1, CHUNK, gn), lambda b, c: (b, c, D_INNER // gn + 1)),
                  pl.BlockSpec((1, CHUNK, LANES), lambda b, c: (b, c, 0)),
                  vec(LANES), vec(LANES), vec(D_INNER),
                  pl.BlockSpec((CHUNK, CHUNK), lambda b, c: (0, 0)),
                  pl.BlockSpec((LANES, D_INNER), lambda b, c: (0, 0))],
        out_specs=pl.BlockSpec((1, CHUNK, D_INNER), lambda b, c: (b, c, 0)),
        out_shape=jax.ShapeDtypeStruct((BATCH, LP, D_INNER), _BF16),
        scratch_shapes=[pltpu.VMEM((N_HEAD_PAIRS, D_STATE, LANES), _F32)],
        compiler_params=_params("arbitrary", "arbitrary"),
        name="ssd",
    )(xbc, xbc, xbc, dt_raw, dtb, aneg, dskip, ltri, expand)


def _mamba_out_kernel(y_ref, z_ref, nw_ref, w_ref, h_ref, g_ref, b_ref, o_ref):
    y = y_ref[...].astype(_F32)
    z = z_ref[...].astype(_F32)
    yg = y * (z * _sigmoid(z))
    gw = D_INNER // SSM_GROUPS
    parts = []
    for g in range(SSM_GROUPS):
        blk = yg[:, g * gw:(g + 1) * gw]
        ms = jnp.mean(blk * blk, axis=-1, keepdims=True)
        parts.append(blk * lax.rsqrt(ms + LN_EPS))
    yn = jnp.concatenate(parts, axis=1) * nw_ref[...]
    mix = jnp.dot(yn.astype(_BF16), w_ref[...], preferred_element_type=_F32)
    o_ref[...] = _layer_norm(ALPHA * h_ref[...] + mix, g_ref[...], b_ref[...])


def _mamba_out(y, proj, norm_w, w_out, h, g, b, tm=256):
    n = h.shape[0]
    vec = lambda d: pl.BlockSpec((1, d), lambda i: (0, 0))
    return pl.pallas_call(
        _mamba_out_kernel,
        grid=(n // tm,),
        in_specs=[pl.BlockSpec((tm, D_INNER), lambda i: (i, 0)),
                  pl.BlockSpec((tm, D_INNER), lambda i: (i, 0)),
                  vec(D_INNER),
                  pl.BlockSpec((D_INNER, D_MODEL), lambda i: (0, 0)),
                  pl.BlockSpec((tm, D_MODEL), lambda i: (i, 0)),
                  vec(D_MODEL), vec(D_MODEL)],
        out_specs=pl.BlockSpec((tm, D_MODEL), lambda i: (i, 0)),
        out_shape=jax.ShapeDtypeStruct((n, D_MODEL), _F32),
        compiler_params=_params("arbitrary"),
        name="mamba_out",
    )(y, proj, norm_w.reshape(1, D_INNER).astype(_F32), w_out, h,
      g.reshape(1, D_MODEL), b.reshape(1, D_MODEL))


def _attn_out_kernel(o_ref_in, w_ref, h_ref, g_ref, b_ref, o_ref):
    mix = jnp.dot(o_ref_in[...], w_ref[...], preferred_element_type=_F32)
    o_ref[...] = _layer_norm(ALPHA * h_ref[...] + mix, g_ref[...], b_ref[...])


def _attn_out(o, w_o, h, g, b, tm=512):
    n = h.shape[0]
    vec = pl.BlockSpec((1, D_MODEL), lambda i: (0, 0))
    return pl.pallas_call(
        _attn_out_kernel,
        grid=(n // tm,),
        in_specs=[pl.BlockSpec((tm, SB_WIDTH), lambda i: (i, 0)),
                  pl.BlockSpec((SB_WIDTH, D_MODEL), lambda i: (0, 0)),
                  pl.BlockSpec((tm, D_MODEL), lambda i: (i, 0)),
                  vec, vec],
        out_specs=pl.BlockSpec((tm, D_MODEL), lambda i: (i, 0)),
        out_shape=jax.ShapeDtypeStruct((n, D_MODEL), _F32),
        compiler_params=_params("arbitrary"),
        name="attn_out",
    )(o, w_o, h, g.reshape(1, D_MODEL), b.reshape(1, D_MODEL))


SB_TQ = 128
SB_TK = 256
KV_FRONT = SB_TK - SB_TQ
LKV = LP + KV_FRONT


SB_HPS = 4
SB_LW = SB_HPS * SB_HEAD_DIM
N_QT = LP // SB_TQ
MASKED_SCORE = -1e30


def _sb_items():
    qi_l, kb_l, first_l = [], [], []
    for qi in range(N_QT):
        n_kb = -(-(qi * SB_TQ + SB_TQ + KV_FRONT) // SB_TK)
        for t in range(n_kb):
            qi_l.append(qi)
            kb_l.append(n_kb - 1 - t)
            first_l.append(int(t == 0))
    return qi_l, kb_l, first_l


N_SB_ITEMS = len(_sb_items()[0])


def _sb_kernel(qi_tab, kb_tab, first_tab, q_ref, k_ref, v_ref, tri_ref, o_ref,
               lb_ref, hi_ref, lo_ref, w_ref, rs_ref, tot_ref):
    tri = tri_ref[...]
    q_head = lax.broadcasted_iota(jnp.int32, (SB_TQ, SB_LW), 1) // SB_HEAD_DIM
    v_head = lax.broadcasted_iota(jnp.int32, (SB_TK, SB_LW), 1) // SB_HEAD_DIM
    col_minus_row = (lax.broadcasted_iota(jnp.int32, (SB_TQ, SB_TK), 1)
                     - lax.broadcasted_iota(jnp.int32, (SB_TQ, SB_TK), 0))

    def stage_a(i, slot):
        qi, kb = qi_tab[i], kb_tab[i]
        q = q_ref[0, pl.ds(pl.multiple_of(qi * SB_TQ, SB_TQ), SB_TQ), :]
        kblk = k_ref[0, pl.ds(pl.multiple_of(kb * SB_TK, SB_TK), SB_TK), :]
        valid = col_minus_row < qi * SB_TQ + KV_FRONT - kb * SB_TK
        zero_q = jnp.zeros_like(q)
        for h in range(SB_HPS):
            z = lax.dot_general(jnp.where(q_head == h, q, zero_q), kblk, (((1,), (1,)), ((), ())),
                                preferred_element_type=_F32)
            z = jnp.where(valid, z, MASKED_SCORE)
            soft = jnp.log(1.0 + jnp.exp(-jnp.abs(z)))
            log_beta = jnp.minimum(z, 0.0) - soft
            log_keep = log_beta - z
            hi = log_keep.astype(_BF16)
            lb_ref[slot, h] = log_beta
            hi_ref[slot, h] = hi
            lo_ref[slot, h] = (log_keep - hi.astype(_F32)).astype(_BF16)
            rs_ref[slot, h] = jnp.broadcast_to(jnp.sum(log_keep, axis=-1, keepdims=True),
                                               (SB_TQ, LANES))

    def stage_b(slot):
        for h in range(SB_HPS):
            later = (jnp.dot(hi_ref[slot, h], tri, preferred_element_type=_F32)
                     + jnp.dot(lo_ref[slot, h], tri, preferred_element_type=_F32))
            total = jnp.tile(tot_ref[h], (1, SB_TK // LANES))
            w_ref[slot, h] = jnp.exp(lb_ref[slot, h] + later + total).astype(_BF16)

    def next_totals(i, prev_slot):
        fresh = first_tab[i] == 1
        for h in range(SB_HPS):
            tot_ref[h] = jnp.where(fresh, 0.0, tot_ref[h] + rs_ref[prev_slot, h])

    def stage_c(j, slot, acc):
        qi, kb = qi_tab[j], kb_tab[j]
        vblk = v_ref[0, pl.ds(pl.multiple_of(kb * SB_TK, SB_TK), SB_TK), :]
        zero_v = jnp.zeros_like(vblk)
        acc = jnp.where(first_tab[j] == 1, 0.0, acc)
        for h in range(SB_HPS):
            acc = acc + jnp.dot(w_ref[slot, h], jnp.where(v_head == h, vblk, zero_v),
                                preferred_element_type=_F32)
        o_ref[0, pl.ds(pl.multiple_of(qi * SB_TQ, SB_TQ), SB_TQ), :] = acc.astype(o_ref.dtype)
        return acc

    def step(i, slot, acc):
        acc = stage_c(i - 2, slot, acc)
        stage_b(1 - slot)
        next_totals(i, 1 - slot)
        stage_a(i, slot)
        return acc

    def two_steps(pair, acc):
        i = 2 + 2 * pair
        return step(i + 1, 1, step(i, 0, acc))

    tot_ref[...] = jnp.zeros_like(tot_ref)
    stage_a(0, 0)
    stage_b(0)
    next_totals(1, 0)
    stage_a(1, 1)
    n_pairs = (N_SB_ITEMS - 2) // 2
    acc = lax.fori_loop(0, n_pairs, two_steps, jnp.zeros((SB_TQ, SB_LW), _F32))
    done = 2 + 2 * n_pairs
    for i in range(done, N_SB_ITEMS):
        acc = step(i, i % 2, acc)
    acc = stage_c(N_SB_ITEMS - 2, N_SB_ITEMS % 2, acc)
    stage_b((N_SB_ITEMS - 1) % 2)
    stage_c(N_SB_ITEMS - 1, (N_SB_ITEMS - 1) % 2, acc)


def _stick_breaking(q, kv):
    j = jnp.arange(SB_TK)
    tri = (j[:, None] > j[None, :]).astype(_BF16)
    qi_l, kb_l, first_l = _sb_items()
    tabs = [jnp.array(t, jnp.int32) for t in (qi_l, kb_l, first_l)]
    n_steps = SB_WIDTH // SB_LW
    slots = lambda dtype: pltpu.VMEM((2, SB_HPS, SB_TQ, SB_TK), dtype)
    grid_spec = pltpu.PrefetchScalarGridSpec(
        num_scalar_prefetch=3,
        grid=(BATCH, n_steps),
        in_specs=[pl.BlockSpec((1, LP, SB_LW), lambda b, p, *_: (b, 0, p)),
                  pl.BlockSpec((1, LKV, SB_LW), lambda b, p, *_: (b, 0, p)),
                  pl.BlockSpec((1, LKV, SB_LW), lambda b, p, *_: (b, 0, p + n_steps)),
                  pl.BlockSpec((SB_TK, SB_TK), lambda b, p, *_: (0, 0))],
        out_specs=pl.BlockSpec((1, LP, SB_LW), lambda b, p, *_: (b, 0, p)),
        scratch_shapes=[slots(_F32), slots(_BF16), slots(_BF16), slots(_BF16),
                        pltpu.VMEM((2, SB_HPS, SB_TQ, LANES), _F32),
                        pltpu.VMEM((SB_HPS, SB_TQ, LANES), _F32)],
    )
    return pl.pallas_call(
        _sb_kernel,
        grid_spec=grid_spec,
        out_shape=jax.ShapeDtypeStruct((BATCH, LP, SB_WIDTH), _BF16),
        compiler_params=_params("arbitrary", "arbitrary"),
        name="stick_breaking",
    )(*tabs, q, kv, kv, tri)


ROUTER_TM = 512


def _router_kernel(h_ref, w_ref, b_ref, cls_ref, glo_ref, ghi_ref):
    logits = lax.dot_general(w_ref[...], h_ref[...], (((1,), (1,)), ((), ())),
                             preferred_element_type=_F32,
                             precision=lax.Precision.HIGHEST) + b_ref[...]
    m = jnp.max(logits, axis=0, keepdims=True)
    e = jnp.exp(logits - m)
    p = e / jnp.sum(e, axis=0, keepdims=True)
    ng = N_EXPERT_GROUPS
    pj = [p[j * ng:(j + 1) * ng, :] for j in range(EXPERTS_PER_GROUP)]
    m1 = jnp.maximum(jnp.maximum(pj[0], pj[1]), jnp.maximum(pj[2], pj[3]))
    i1 = jnp.where(pj[0] == m1, 0, jnp.where(pj[1] == m1, 1, jnp.where(pj[2] == m1, 2, 3)))
    qj = [jnp.where(i1 == j, -1.0, pj[j]) for j in range(EXPERTS_PER_GROUP)]
    m2 = jnp.maximum(jnp.maximum(qj[0], qj[1]), jnp.maximum(qj[2], qj[3]))
    i2 = jnp.where(qj[0] == m2, 0, jnp.where(qj[1] == m2, 1, jnp.where(qj[2] == m2, 2, 3)))
    score = m1 + m2
    gid = lax.broadcasted_iota(jnp.int32, score.shape, 0)
    best = jnp.max(score, axis=0, keepdims=True)
    gsel = jnp.min(jnp.where(score == best, gid, ng), axis=0, keepdims=True)
    chosen = gid == gsel
    pick_f = lambda v: jnp.sum(jnp.where(chosen, v, 0.0), axis=0, keepdims=True)
    pick_i = lambda v: jnp.sum(jnp.where(chosen, v, 0), axis=0, keepdims=True)
    v1, v2, j1, j2 = pick_f(m1), pick_f(m2), pick_i(i1), pick_i(i2)
    denom = v1 + v2
    g1 = v1 / denom
    g2 = v2 / denom
    first_low = j1 < j2
    lo = jnp.where(first_low, j1, j2)
    hi = jnp.where(first_low, j2, j1)
    base = jnp.where(lo == 0, 0, jnp.where(lo == 1, 3, 5))
    cls_ref[...] = gsel * N_PAIR_CLASSES + base + hi - lo - 1
    glo_ref[...] = jnp.where(first_low, g1, g2)
    ghi_ref[...] = jnp.where(first_low, g2, g1)


def _router(h, router_w, router_b):
    n = h.shape[0]
    perm = jnp.array([g * EXPERTS_PER_GROUP + j for j in range(EXPERTS_PER_GROUP)
                      for g in range(N_EXPERT_GROUPS)])
    w_t = router_w.astype(_F32).T[perm]
    b_t = router_b.astype(_F32)[perm].reshape(N_EXPERTS, 1)
    out = pl.BlockSpec((1, ROUTER_TM), lambda i: (0, i))
    return pl.pallas_call(
        _router_kernel,
        grid=(n // ROUTER_TM,),
        in_specs=[pl.BlockSpec((ROUTER_TM, D_MODEL), lambda i: (i, 0)),
                  pl.BlockSpec((N_EXPERTS, D_MODEL), lambda i: (0, 0)),
                  pl.BlockSpec((N_EXPERTS, 1), lambda i: (0, 0))],
        out_specs=[out, out, out],
        out_shape=[jax.ShapeDtypeStruct((1, n), jnp.int32),
                   jax.ShapeDtypeStruct((1, n), _F32),
                   jax.ShapeDtypeStruct((1, n), _F32)],
        compiler_params=_params("arbitrary"),
        name="router",
    )(h, w_t, b_t)


def _class_experts():
    lo, hi = [], []
    for g in range(N_EXPERT_GROUPS):
        for a in range(EXPERTS_PER_GROUP):
            for b in range(a + 1, EXPERTS_PER_GROUP):
                lo.append(g * EXPERTS_PER_GROUP + a)
                hi.append(g * EXPERTS_PER_GROUP + b)
    return jnp.array(lo, jnp.int32), jnp.array(hi, jnp.int32)


def _expert_kernel(elo_ref, ehi_ref, nused_ref, x_ref, wg1, wu1, wd1, wg2, wu2, wd2,
                   g1_ref, g2_ref, o_ref):
    t = pl.program_id(0)

    @pl.when(t < nused_ref[0])
    def _():
        x = x_ref[...]

        def ffn(wg, wu, wd, gate_ref):
            a = jnp.dot(x, wg[0], preferred_element_type=_F32)
            u = jnp.dot(x, wu[0], preferred_element_type=_F32)
            gate = jnp.tile(gate_ref[...], (1, D_EXPERT // LANES))
            hid = (a * _sigmoid(a)) * u * gate
            return jnp.dot(hid.astype(_BF16), wd[0], preferred_element_type=_F32)

        o_ref[...] = ffn(wg1, wu1, wd1, g1_ref) + ffn(wg2, wu2, wd2, g2_ref)

    @pl.when(t >= nused_ref[0])
    def _():
        o_ref[...] = jnp.zeros_like(o_ref)


def _experts(x_sorted, g_lo, g_hi, tile_lo, tile_hi, n_used, w_gate, w_up, w_down):
    up_spec = lambda which: pl.BlockSpec(
        (1, D_MODEL, D_EXPERT), lambda t, lo, hi, nu: ((lo, hi)[which][t], 0, 0))
    down_spec = lambda which: pl.BlockSpec(
        (1, D_EXPERT, D_MODEL), lambda t, lo, hi, nu: ((lo, hi)[which][t], 0, 0))
    row = lambda d: pl.BlockSpec((MOE_TM, d), lambda t, lo, hi, nu: (t, 0))
    grid_spec = pltpu.PrefetchScalarGridSpec(
        num_scalar_prefetch=3,
        grid=(MOE_TILES,),
        in_specs=[row(D_MODEL), up_spec(0), up_spec(0), down_spec(0),
                  up_spec(1), up_spec(1), down_spec(1), row(LANES), row(LANES)],
        out_specs=row(D_MODEL),
    )
    return pl.pallas_call(
        _expert_kernel,
        grid_spec=grid_spec,
        out_shape=jax.ShapeDtypeStruct((MOE_TILES * MOE_TM, D_MODEL), _F32),
        compiler_params=_params("arbitrary"),
        name="experts",
    )(tile_lo, tile_hi, n_used, x_sorted, w_gate, w_up, w_down, w_gate, w_up, w_down, g_lo, g_hi)


def _routed_moe(h, router_w, router_b, w_gate, w_up, w_down):
    cls, glo, ghi = _router(h, router_w, router_b)
    cls, glo, ghi = cls[0], glo[0], ghi[0]
    order = jnp.argsort(cls, stable=True).astype(jnp.int32)
    sorted_cls = cls[order]
    counts = jnp.bincount(cls, length=N_CLASSES).astype(jnp.int32)
    starts = jnp.cumsum(counts) - counts
    tiles = (counts + MOE_TM - 1) // MOE_TM
    tile_end = jnp.cumsum(tiles)
    padded_start = (tile_end - tiles) * MOE_TM
    dest_sorted = padded_start[sorted_cls] + jnp.arange(NP, dtype=jnp.int32) - starts[sorted_cls]
    total_rows = MOE_TILES * MOE_TM
    src = jnp.zeros((total_rows,), jnp.int32).at[dest_sorted].set(order)
    occupied = jnp.zeros((total_rows,), _F32).at[dest_sorted].set(1.0)
    pos = jnp.zeros((NP,), jnp.int32).at[order].set(dest_sorted)
    n_used = tile_end[-1]
    tile_ids = jnp.minimum(jnp.arange(MOE_TILES, dtype=jnp.int32), n_used - 1)
    tile_cls = jnp.searchsorted(tile_end, tile_ids, side="right").astype(jnp.int32)
    class_lo, class_hi = _class_experts()
    x_sorted = h.astype(_BF16)[src]
    g_lo = jnp.broadcast_to((glo[src] * occupied)[:, None], (total_rows, LANES))
    g_hi = jnp.broadcast_to((ghi[src] * occupied)[:, None], (total_rows, LANES))
    y_sorted = _experts(x_sorted, g_lo, g_hi, class_lo[tile_cls], class_hi[tile_cls],
                        n_used.reshape(1).astype(jnp.int32), w_gate, w_up, w_down)
    return y_sorted[pos]


def kernel(x, meta_tokens, mamba_w_in, mamba_conv_w, mamba_conv_b, mamba_dt_bias, mamba_a_log,
           mamba_d_skip, mamba_norm_w, mamba_w_out, sb_w_q, sb_w_o, shared_w_k, shared_w_v,
           ln_mix_g, ln_mix_b, ln_ffn_g, ln_ffn_b, router_w, router_b, moe_w_gate, moe_w_up,
           moe_w_down):
    bsz = x.shape[0]
    meta = jnp.broadcast_to(meta_tokens.astype(x.dtype)[None], (bsz, N_META, D_MODEL))
    h = jnp.concatenate([jnp.zeros((bsz, PAD, D_MODEL), x.dtype), meta, x], axis=1)
    h = h.reshape(NP, D_MODEL)
    zx_dim = D_INNER + CONV_DIM
    kv = None
    for layer in range(DEPTH):
        if layer < N_A_LAYERS:
            w_in = mamba_w_in[layer].astype(_BF16)
            w_dt = jnp.pad(w_in[:, zx_dim:], ((0, 0), (0, LANES - SSM_HEADS)))
            proj = _matmul(h, w_in, zx_dim, 512, 1024, _BF16)
            dt_raw = _matmul(h, w_dt, LANES, 512, LANES, _F32)
            xbc = _conv_silu(proj.reshape(BATCH, LP, zx_dim), mamba_conv_w[layer].astype(_F32),
                             mamba_conv_b[layer].astype(_F32))
            y = _ssd(xbc, dt_raw.reshape(BATCH, LP, LANES), mamba_dt_bias[layer],
                     mamba_a_log[layer], mamba_d_skip[layer])
            h = _mamba_out(y.reshape(NP, D_INNER), proj, mamba_norm_w[layer],
                           mamba_w_out[layer].astype(_BF16), h, ln_mix_g[layer], ln_mix_b[layer])
        else:
            j = layer - N_A_LAYERS
            if kv is None:
                w_kv = jnp.concatenate([shared_w_k, shared_w_v], axis=1).astype(_BF16)
                kv = _matmul(h, w_kv, 2 * SB_WIDTH, 512, 1024, _BF16).reshape(BATCH, LP, 2 * SB_WIDTH)
                kv = jnp.pad(kv[:, PAD:], ((0, 0), (PAD + KV_FRONT, 0), (0, 0)))
            q = _matmul(h, (sb_w_q[j] * SB_SCALE).astype(_BF16), SB_WIDTH, 512, 1024, _BF16)
            o = _stick_breaking(q.reshape(BATCH, LP, SB_WIDTH), kv)
            h = _attn_out(o.reshape(NP, SB_WIDTH), sb_w_o[j].astype(_BF16), h,
                          ln_mix_g[layer], ln_mix_b[layer])
        ffn = _routed_moe(h, router_w, router_b, moe_w_gate[layer].astype(_BF16),
                          moe_w_up[layer].astype(_BF16), moe_w_down[layer].astype(_BF16))
        h = _add_ln(h, ffn, ln_ffn_g[layer], ln_ffn_b[layer])
    return h.reshape(bsz, LP, D_MODEL)[:, PAD + N_META:]
```

```python
import functools
import math

import jax
import jax.numpy as jnp
from jax import lax
from jax.experimental import pallas as pl
from jax.experimental.pallas import tpu as pltpu

D_MODEL = 1024
BATCH = 8
SEQ = 2048
DEPTH = 4
N_META = 16
N_A_LAYERS = DEPTH // 2
ALPHA = (2.0 * DEPTH) ** 0.25
LN_EPS = 1e-5
D_INNER = 2048
SSM_HEAD_DIM = 64
SSM_HEADS = 32
SSM_GROUPS = 4
D_STATE = 128
CONV_WIDTH = 4
CHUNK = 128
CONV_DIM = D_INNER + 2 * SSM_GROUPS * D_STATE
SB_HEADS = 16
SB_HEAD_DIM = 64
SB_WIDTH = SB_HEADS * SB_HEAD_DIM
SB_SCALE = SB_HEAD_DIM ** -0.5
LOG2_E = math.log2(math.e)
N_EXPERTS = 16
N_EXPERT_GROUPS = 4
EXPERTS_PER_GROUP = 4
D_EXPERT = 512

L_REAL = N_META + SEQ
PAD = (-L_REAL) % CHUNK
LP = L_REAL + PAD
N_CHUNKS = LP // CHUNK
NP = BATCH * LP
LANES = 128
N_PAIR_CLASSES = 6
N_CLASSES = N_EXPERT_GROUPS * N_PAIR_CLASSES
MOE_TM = 256
MOE_TILES = NP // MOE_TM + N_CLASSES
MOE_ROWS = MOE_TILES * MOE_TM
CLS_ROWS = 32
XROW = D_MODEL + LANES
ROW_TM = 256
VMEM_LIMIT = 48 * 1024 * 1024

_F32 = jnp.float32
_BF16 = jnp.bfloat16


def _params(*sem):
    return pltpu.CompilerParams(dimension_semantics=sem, vmem_limit_bytes=VMEM_LIMIT)


def _sigmoid(x):
    return 1.0 / (1.0 + jnp.exp(-x))


def _softplus(x):
    return jnp.maximum(x, 0.0) + jnp.log(1.0 + jnp.exp(-jnp.abs(x)))


def _layer_norm(t, g, b):
    mu = jnp.mean(t, axis=-1, keepdims=True)
    d = t - mu
    var = jnp.mean(d * d, axis=-1, keepdims=True)
    return d * lax.rsqrt(var + LN_EPS) * g + b


def _mm_kernel(x_ref, w_ref, o_ref, *, scale):
    x = x_ref[...].astype(_BF16)
    acc = jnp.dot(x, w_ref[...], preferred_element_type=_F32)
    if scale != 1.0:
        acc = acc * scale
    o_ref[...] = acc.astype(o_ref.dtype)


def _matmul(x, w, n_out, tm, tn, out_dtype, scale=1.0):
    m, k = x.shape
    return pl.pallas_call(
        functools.partial(_mm_kernel, scale=scale),
        grid=(n_out // tn, m // tm),
        in_specs=[pl.BlockSpec((tm, k), lambda j, i: (i, 0)),
                  pl.BlockSpec((k, tn), lambda j, i: (0, j))],
        out_specs=pl.BlockSpec((tm, tn), lambda j, i: (i, j)),
        out_shape=jax.ShapeDtypeStruct((m, n_out), out_dtype),
        compiler_params=_params("arbitrary", "arbitrary"),
        name="matmul",
    )(x, w)


def _add_ln_kernel(h_ref, m_ref, g_ref, b_ref, o_ref):
    t = ALPHA * h_ref[...] + m_ref[...]
    o_ref[...] = _layer_norm(t, g_ref[...], b_ref[...])


def _add_ln(h, mix, g, b, tm=512):
    n, d = h.shape
    row = pl.BlockSpec((tm, d), lambda i: (i, 0))
    vec = pl.BlockSpec((1, d), lambda i: (0, 0))
    return pl.pallas_call(
        _add_ln_kernel,
        grid=(n // tm,),
        in_specs=[row, row, vec, vec],
        out_specs=row,
        out_shape=jax.ShapeDtypeStruct((n, d), _F32),
        compiler_params=_params("arbitrary"),
        name="add_ln",
    )(h, mix, g.reshape(1, d), b.reshape(1, d))


CONV_TN = 512
CONV_HALO = 16


def _conv_kernel(x_ref, w_ref, b_ref, o_ref):
    w = w_ref[...]
    bias = b_ref[...]
    for c in range(N_CHUNKS):
        s = c * CHUNK
        if c == 0:
            cur = x_ref[0, 0:CHUNK, :].astype(_F32)
            rows = lax.broadcasted_iota(jnp.int32, cur.shape, 0)
            cur = jnp.where(rows >= PAD, cur, 0.0)
            xx = jnp.concatenate([jnp.zeros((CONV_HALO, CONV_TN), _F32), cur], axis=0)
        else:
            xx = x_ref[0, s - CONV_HALO:s + CHUNK, :].astype(_F32)
            if s - CONV_HALO < PAD:
                rows = lax.broadcasted_iota(jnp.int32, xx.shape, 0) + (s - CONV_HALO)
                xx = jnp.where(rows >= PAD, xx, 0.0)
        acc = bias + w[CONV_WIDTH - 1:CONV_WIDTH, :] * xx[CONV_HALO:, :]
        for back in range(1, CONV_WIDTH):
            shifted = pltpu.roll(xx, back, 0)
            k = CONV_WIDTH - 1 - back
            acc = acc + w[k:k + 1, :] * shifted[CONV_HALO:, :]
        o_ref[0, s:s + CHUNK, :] = (acc * _sigmoid(acc)).astype(o_ref.dtype)


def _conv_silu(proj, conv_w, conv_b):
    first = D_INNER // CONV_TN
    return pl.pallas_call(
        _conv_kernel,
        grid=(BATCH, CONV_DIM // CONV_TN),
        in_specs=[pl.BlockSpec((1, LP, CONV_TN), lambda b, j: (b, 0, first + j)),
                  pl.BlockSpec((CONV_WIDTH, CONV_TN), lambda b, j: (0, j)),
                  pl.BlockSpec((1, CONV_TN), lambda b, j: (0, j))],
        out_specs=pl.BlockSpec((1, LP, CONV_TN), lambda b, j: (b, 0, j)),
        out_shape=jax.ShapeDtypeStruct((BATCH, LP, CONV_DIM), _BF16),
        compiler_params=_params("arbitrary", "arbitrary"),
        name="conv_silu",
    )(proj, conv_w, conv_b.reshape(1, CONV_DIM))


N_HEAD_PAIRS = SSM_HEADS // 2
PAIRS_PER_GROUP = N_HEAD_PAIRS // SSM_GROUPS


def _ssd_kernel(xs_ref, b_ref, c_ref, dtr_ref, dtb_ref, aneg_ref, dskip_ref, ltri_ref, exp_ref,
                y_ref, state_ref):
    c = pl.program_id(1)

    @pl.when(c == 0)
    def _():
        state_ref[...] = jnp.zeros_like(state_ref)

    rows = lax.broadcasted_iota(jnp.int32, (CHUNK, LANES), 0)
    cols = lax.broadcasted_iota(jnp.int32, (CHUNK, LANES), 1)
    causal = cols <= rows
    left = cols < SSM_HEAD_DIM

    dt = _softplus(dtr_ref[0] + dtb_ref[...])
    dt = jnp.where(rows + c * CHUNK >= PAD, dt, 0.0)
    a = dt * aneg_ref[...]
    a_cum = jnp.dot(ltri_ref[...], a, preferred_element_type=_F32,
                    precision=lax.Precision.HIGHEST)
    a_cum_t = a_cum.T
    total = a_cum[CHUNK - 1:CHUNK, :]
    from_start = jnp.exp(a_cum)
    to_end = jnp.exp(total - a_cum)
    chunk_decay = jnp.exp(jnp.broadcast_to(total, (8, LANES)))

    expand = exp_ref[...]
    dt_x = jnp.dot(dt.astype(_BF16), expand, preferred_element_type=_F32)
    to_end_x = jnp.dot(to_end.astype(_BF16), expand, preferred_element_type=_F32)
    chunk_decay_x = jnp.dot(chunk_decay.astype(_BF16), expand, preferred_element_type=_F32)

    for g in range(SSM_GROUPS):
        bg = b_ref[0, :, g * D_STATE:(g + 1) * D_STATE]
        cg = c_ref[0, :, g * D_STATE:(g + 1) * D_STATE]
        cb = lax.dot_general(cg, bg, (((1,), (1,)), ((), ())), preferred_element_type=_F32)
        bg_t = bg.astype(_F32).T.astype(_BF16)
        cg32 = cg.astype(_F32)
        for jj in range(PAIRS_PER_GROUP):
            j = g * PAIRS_PER_GROUP + jj
            sl = slice(j * LANES, (j + 1) * LANES)
            x2 = xs_ref[0, :, sl].astype(_F32)
            xdt = x2 * dt_x[:, sl]
            parts = []
            for h in (2 * j, 2 * j + 1):
                seg = a_cum[:, h:h + 1] - a_cum_t[h:h + 1, :]
                decay = jnp.where(causal, jnp.exp(seg), 0.0)
                parts.append((cb * decay).astype(_BF16))
            for h in (2 * j, 2 * j + 1):
                parts.append((cg32 * from_start[:, h:h + 1]).astype(_BF16))
            lhs = jnp.concatenate(parts, axis=1)
            state = state_ref[j]
            xb = xdt.astype(_BF16)
            sb = state.astype(_BF16)
            zero = jnp.zeros_like(xb)
            rhs = jnp.concatenate([jnp.where(left, xb, zero), jnp.where(left, zero, xb),
                                   jnp.where(left, sb, zero), jnp.where(left, zero, sb)], axis=0)
            y2 = jnp.dot(lhs, rhs, preferred_element_type=_F32)
            y2 = y2 + dskip_ref[:, sl] * x2
            y_ref[0, :, sl] = y2.astype(y_ref.dtype)
            upd = jnp.dot(bg_t, (xdt * to_end_x[:, sl]).astype(_BF16), preferred_element_type=_F32)
            state_ref[j] = state * chunk_decay_x[0:1, sl] + upd


def _ssd(xbc, dt_raw, dt_bias, a_log, d_skip):
    pad_h = LANES - SSM_HEADS
    dtb = jnp.pad(dt_bias.astype(_F32), (0, pad_h)).reshape(1, LANES)
    aneg = jnp.pad(-jnp.exp(a_log.astype(_F32)), (0, pad_h)).reshape(1, LANES)
    dskip = jnp.repeat(d_skip.astype(_F32), SSM_HEAD_DIM).reshape(1, D_INNER)
    ltri = jnp.tril(jnp.ones((CHUNK, CHUNK), _F32))
    expand = (jnp.arange(LANES)[:, None] == (jnp.arange(D_INNER)[None, :] // SSM_HEAD_DIM)).astype(_BF16)
    gn = SSM_GROUPS * D_STATE
    vec = lambda n: pl.BlockSpec((1, n), lambda b, c: (0, 0))
    return pl.pallas_call(
        _ssd_kernel,
        grid=(BATCH, N_CHUNKS),
        in_specs=[pl.BlockSpec((1, CHUNK, D_INNER), lambda b, c: (b, c, 0)),
                  pl.BlockSpec((1, CHUNK, gn), lambda b, c: (b, c, D_INNER // gn)),
                  pl.BlockSpec((1, CHUNK, gn), lambda b, c: (b, c, D_INNER // gn + 1)),
                  pl.BlockSpec((1, CHUNK, LANES), lambda b, c: (b, c, 0)),
                  vec(LANES), vec(LANES), vec(D_INNER),
                  pl.BlockSpec((CHUNK, CHUNK), lambda b, c: (0, 0)),
                  pl.BlockSpec((LANES, D_INNER), lambda b, c: (0, 0))],
        out_specs=pl.BlockSpec((1, CHUNK, D_INNER), lambda b, c: (b, c, 0)),
        out_shape=jax.ShapeDtypeStruct((BATCH, LP, D_INNER), _BF16),
        scratch_shapes=[pltpu.VMEM((N_HEAD_PAIRS, D_STATE, LANES), _F32)],
        compiler_params=_params("arbitrary", "arbitrary"),
        name="ssd",
    )(xbc, xbc, xbc, dt_raw, dtb, aneg, dskip, ltri, expand)


def _mamba_out_kernel(y_ref, z_ref, nw_ref, w_ref, h_ref, g_ref, b_ref, o_ref):
    y = y_ref[...].astype(_F32)
    z = z_ref[...].astype(_F32)
    yg = y * (z * _sigmoid(z))
    gw = D_INNER // SSM_GROUPS
    parts = []
    for g in range(SSM_GROUPS):
        blk = yg[:, g * gw:(g + 1) * gw]
        ms = jnp.mean(blk * blk, axis=-1, keepdims=True)
        parts.append(blk * lax.rsqrt(ms + LN_EPS))
    yn = jnp.concatenate(parts, axis=1) * nw_ref[...]
    mix = jnp.dot(yn.astype(_BF16), w_ref[...], preferred_element_type=_F32)
    o_ref[...] = _layer_norm(ALPHA * h_ref[...] + mix, g_ref[...], b_ref[...])


def _mamba_out(y, proj, norm_w, w_out, h, g, b, tm=256):
    n = h.shape[0]
    vec = lambda d: pl.BlockSpec((1, d), lambda i: (0, 0))
    return pl.pallas_call(
        _mamba_out_kernel,
        grid=(n // tm,),
        in_specs=[pl.BlockSpec((tm, D_INNER), lambda i: (i, 0)),
                  pl.BlockSpec((tm, D_INNER), lambda i: (i, 0)),
                  vec(D_INNER),
                  pl.BlockSpec((D_INNER, D_MODEL), lambda i: (0, 0)),
                  pl.BlockSpec((tm, D_MODEL), lambda i: (i, 0)),
                  vec(D_MODEL), vec(D_MODEL)],
        out_specs=pl.BlockSpec((tm, D_MODEL), lambda i: (i, 0)),
        out_shape=jax.ShapeDtypeStruct((n, D_MODEL), _F32),
        compiler_params=_params("arbitrary"),
        name="mamba_out",
    )(y, proj, norm_w.reshape(1, D_INNER).astype(_F32), w_out, h,
      g.reshape(1, D_MODEL), b.reshape(1, D_MODEL))


def _attn_out_kernel(o_ref_in, w_ref, h_ref, g_ref, b_ref, o_ref):
    mix = jnp.dot(o_ref_in[...], w_ref[...], preferred_element_type=_F32)
    o_ref[...] = _layer_norm(ALPHA * h_ref[...] + mix, g_ref[...], b_ref[...])


def _attn_out(o, w_o, h, g, b, tm=512):
    n = h.shape[0]
    vec = pl.BlockSpec((1, D_MODEL), lambda i: (0, 0))
    return pl.pallas_call(
        _attn_out_kernel,
        grid=(n // tm,),
        in_specs=[pl.BlockSpec((tm, SB_WIDTH), lambda i: (i, 0)),
                  pl.BlockSpec((SB_WIDTH, D_MODEL), lambda i: (0, 0)),
                  pl.BlockSpec((tm, D_MODEL), lambda i: (i, 0)),
                  vec, vec],
        out_specs=pl.BlockSpec((tm, D_MODEL), lambda i: (i, 0)),
        out_shape=jax.ShapeDtypeStruct((n, D_MODEL), _F32),
        compiler_params=_params("arbitrary"),
        name="attn_out",
    )(o, w_o, h, g.reshape(1, D_MODEL), b.reshape(1, D_MODEL))


SB_TQ = 128
SB_TK = 256
KV_FRONT = SB_TK - SB_TQ
LKV = LP + KV_FRONT


SB_HPS = 4
SB_LW = SB_HPS * SB_HEAD_DIM
N_QT = LP // SB_TQ
MASKED_SCORE = -1e30


def _sb_items():
    qi_l, kb_l, first_l = [], [], []
    for qi in range(N_QT):
        n_kb = -(-(qi * SB_TQ + SB_TQ + KV_FRONT) // SB_TK)
        for t in range(n_kb):
            qi_l.append(qi)
            kb_l.append(n_kb - 1 - t)
            first_l.append(int(t == 0))
    return qi_l, kb_l, first_l


N_SB_ITEMS = len(_sb_items()[0])


def _sb_kernel(qi_tab, kb_tab, first_tab, q_ref, k_ref, v_ref, tri_ref, o_ref,
               qm_ref, vm_ref, lb_ref, sp_ref, w_ref, rs_ref, tot_ref):
    tri = tri_ref[...]
    col_minus_row = (lax.broadcasted_iota(jnp.int32, (SB_TQ, SB_TK), 1)
                     - lax.broadcasted_iota(jnp.int32, (SB_TQ, SB_TK), 0))

    def mask_q(c, carry):
        rows = pl.ds(pl.multiple_of(c * SB_TQ, SB_TQ), SB_TQ)
        blk = q_ref[0, rows, :]
        head = lax.broadcasted_iota(jnp.int32, blk.shape, 1) // SB_HEAD_DIM
        for h in range(SB_HPS):
            qm_ref[h, rows, :] = jnp.where(head == h, blk, jnp.zeros_like(blk))
        return carry

    def mask_v(c, carry):
        rows = pl.ds(pl.multiple_of(c * SB_TK, SB_TK), SB_TK)
        blk = v_ref[0, rows, :]
        head = lax.broadcasted_iota(jnp.int32, blk.shape, 1) // SB_HEAD_DIM
        for h in range(SB_HPS):
            vm_ref[h, rows, :] = jnp.where(head == h, blk, jnp.zeros_like(blk))
        return carry

    lax.fori_loop(0, N_QT, mask_q, 0)
    lax.fori_loop(0, LKV // SB_TK, mask_v, 0)

    def stage_a(i, slot):
        qi, kb = qi_tab[i], kb_tab[i]
        q_rows = pl.ds(pl.multiple_of(qi * SB_TQ, SB_TQ), SB_TQ)
        kblk = k_ref[0, pl.ds(pl.multiple_of(kb * SB_TK, SB_TK), SB_TK), :]
        valid = col_minus_row < qi * SB_TQ + KV_FRONT - kb * SB_TK
        for h in range(SB_HPS):
            z = lax.dot_general(qm_ref[h, q_rows, :], kblk, (((1,), (1,)), ((), ())),
                                preferred_element_type=_F32)
            z = jnp.where(valid, z, MASKED_SCORE)
            neg_abs = pltpu.bitcast(pltpu.bitcast(z, jnp.uint32) | jnp.uint32(0x80000000), _F32)
            s = jnp.maximum(z, 0.0) + jnp.log2(1.0 + jnp.exp2(neg_abs))
            lb_ref[slot, h] = z - s
            sp_ref[slot, h] = s.astype(_BF16)
            rs_ref[slot, h] = jnp.broadcast_to(jnp.sum(s, axis=-1, keepdims=True), (SB_TQ, LANES))

    def stage_b(slot):
        for h in range(SB_HPS):
            later = jnp.dot(sp_ref[slot, h], tri, preferred_element_type=_F32)
            total = jnp.tile(tot_ref[h], (1, SB_TK // LANES))
            w_ref[slot, h] = jnp.exp2(lb_ref[slot, h] - later - total).astype(_BF16)

    def next_totals(i, prev_slot):
        fresh = first_tab[i] == 1
        for h in range(SB_HPS):
            tot_ref[h] = jnp.where(fresh, 0.0, tot_ref[h] + rs_ref[prev_slot, h])

    def stage_c(j, slot, acc):
        qi, kb = qi_tab[j], kb_tab[j]
        k_rows = pl.ds(pl.multiple_of(kb * SB_TK, SB_TK), SB_TK)
        acc = jnp.where(first_tab[j] == 1, 0.0, acc)
        for h in range(SB_HPS):
            acc = acc + jnp.dot(w_ref[slot, h], vm_ref[h, k_rows, :], preferred_element_type=_F32)
        o_ref[0, pl.ds(pl.multiple_of(qi * SB_TQ, SB_TQ), SB_TQ), :] = acc.astype(o_ref.dtype)
        return acc

    def step(i, slot, acc):
        acc = stage_c(i - 2, slot, acc)
        stage_b(1 - slot)
        next_totals(i, 1 - slot)
        stage_a(i, slot)
        return acc

    def two_steps(pair, acc):
        i = 2 + 2 * pair
        return step(i + 1, 1, step(i, 0, acc))

    tot_ref[...] = jnp.zeros_like(tot_ref)
    stage_a(0, 0)
    stage_b(0)
    next_totals(1, 0)
    stage_a(1, 1)
    n_pairs = (N_SB_ITEMS - 2) // 2
    acc = lax.fori_loop(0, n_pairs, two_steps, jnp.zeros((SB_TQ, SB_LW), _F32))
    done = 2 + 2 * n_pairs
    for i in range(done, N_SB_ITEMS):
        acc = step(i, i % 2, acc)
    acc = stage_c(N_SB_ITEMS - 2, N_SB_ITEMS % 2, acc)
    stage_b((N_SB_ITEMS - 1) % 2)
    stage_c(N_SB_ITEMS - 1, (N_SB_ITEMS - 1) % 2, acc)


def _stick_breaking(q, kv):
    j = jnp.arange(SB_TK)
    tri = (j[:, None] > j[None, :]).astype(_BF16)
    qi_l, kb_l, first_l = _sb_items()
    tabs = [jnp.array(t, jnp.int32) for t in (qi_l, kb_l, first_l)]
    n_steps = SB_WIDTH // SB_LW
    slots = lambda dtype: pltpu.VMEM((2, SB_HPS, SB_TQ, SB_TK), dtype)
    grid_spec = pltpu.PrefetchScalarGridSpec(
        num_scalar_prefetch=3,
        grid=(BATCH, n_steps),
        in_specs=[pl.BlockSpec((1, LP, SB_LW), lambda b, p, *_: (b, 0, p)),
                  pl.BlockSpec((1, LKV, SB_LW), lambda b, p, *_: (b, 0, p)),
                  pl.BlockSpec((1, LKV, SB_LW), lambda b, p, *_: (b, 0, p + n_steps)),
                  pl.BlockSpec((SB_TK, SB_TK), lambda b, p, *_: (0, 0))],
        out_specs=pl.BlockSpec((1, LP, SB_LW), lambda b, p, *_: (b, 0, p)),
        scratch_shapes=[pltpu.VMEM((SB_HPS, LP, SB_LW), _BF16),
                        pltpu.VMEM((SB_HPS, LKV, SB_LW), _BF16),
                        slots(_F32), slots(_BF16), slots(_BF16),
                        pltpu.VMEM((2, SB_HPS, SB_TQ, LANES), _F32),
                        pltpu.VMEM((SB_HPS, SB_TQ, LANES), _F32)],
    )
    return pl.pallas_call(
        _sb_kernel,
        grid_spec=grid_spec,
        out_shape=jax.ShapeDtypeStruct((BATCH, LP, SB_WIDTH), _BF16),
        compiler_params=_params("arbitrary", "arbitrary"),
        name="stick_breaking",
    )(*tabs, q, kv, kv, tri)


ROUTER_TM = 512


def _router_kernel(h_ref, w_ref, b_ref, tri_ref, cls_ref, rank_ref, cnt_ref, gext_ref, base_ref):
    @pl.when(pl.program_id(0) == 0)
    def _():
        base_ref[...] = jnp.zeros_like(base_ref)

    logits = lax.dot_general(w_ref[...], h_ref[...], (((1,), (1,)), ((), ())),
                             preferred_element_type=_F32,
                             precision=lax.Precision.HIGHEST) + b_ref[...]
    m = jnp.max(logits, axis=0, keepdims=True)
    e = jnp.exp(logits - m)
    p = e / jnp.sum(e, axis=0, keepdims=True)
    ng = N_EXPERT_GROUPS
    pj = [p[j * ng:(j + 1) * ng, :] for j in range(EXPERTS_PER_GROUP)]
    m1 = jnp.maximum(jnp.maximum(pj[0], pj[1]), jnp.maximum(pj[2], pj[3]))
    i1 = jnp.where(pj[0] == m1, 0, jnp.where(pj[1] == m1, 1, jnp.where(pj[2] == m1, 2, 3)))
    qj = [jnp.where(i1 == j, -1.0, pj[j]) for j in range(EXPERTS_PER_GROUP)]
    m2 = jnp.maximum(jnp.maximum(qj[0], qj[1]), jnp.maximum(qj[2], qj[3]))
    i2 = jnp.where(qj[0] == m2, 0, jnp.where(qj[1] == m2, 1, jnp.where(qj[2] == m2, 2, 3)))
    score = m1 + m2
    gid = lax.broadcasted_iota(jnp.int32, score.shape, 0)
    best = jnp.max(score, axis=0, keepdims=True)
    gsel = jnp.min(jnp.where(score == best, gid, ng), axis=0, keepdims=True)
    chosen = gid == gsel
    pick_f = lambda v: jnp.sum(jnp.where(chosen, v, 0.0), axis=0, keepdims=True)
    pick_i = lambda v: jnp.sum(jnp.where(chosen, v, 0), axis=0, keepdims=True)
    v1, v2, j1, j2 = pick_f(m1), pick_f(m2), pick_i(i1), pick_i(i2)
    denom = v1 + v2
    g1 = v1 / denom
    g2 = v2 / denom
    first_low = j1 < j2
    lo = jnp.where(first_low, j1, j2)
    hi = jnp.where(first_low, j2, j1)
    base = jnp.where(lo == 0, 0, jnp.where(lo == 1, 3, 5))
    cls = gsel * N_PAIR_CLASSES + base + hi - lo - 1
    cls_ref[...] = cls

    member = lax.broadcasted_iota(jnp.int32, (CLS_ROWS, ROUTER_TM), 0) == cls
    onehot = jnp.where(member, 1.0, 0.0)
    upto = jnp.dot(onehot.astype(_BF16), tri_ref[...], preferred_element_type=_F32)
    before = base_ref[...]
    rank = jnp.sum(jnp.where(member, upto - 1.0 + before, 0.0), axis=0, keepdims=True)
    rank_ref[...] = rank.astype(jnp.int32)
    after = before + jnp.sum(onehot, axis=1, keepdims=True)
    base_ref[...] = after
    cnt_ref[...] = after[:, :LANES]

    gates = jnp.concatenate([jnp.where(first_low, g1, g2), jnp.where(first_low, g2, g1),
                             jnp.zeros((LANES - 2, ROUTER_TM), _F32)], axis=0)
    gext_ref[...] = gates.T


def _router(h, router_w, router_b):
    n = h.shape[0]
    perm = jnp.array([g * EXPERTS_PER_GROUP + j for j in range(EXPERTS_PER_GROUP)
                      for g in range(N_EXPERT_GROUPS)])
    w_t = router_w.astype(_F32).T[perm]
    b_t = router_b.astype(_F32)[perm].reshape(N_EXPERTS, 1)
    j = jnp.arange(ROUTER_TM)
    tri = (j[:, None] <= j[None, :]).astype(_BF16)
    out = pl.BlockSpec((1, ROUTER_TM), lambda i: (0, i))
    return pl.pallas_call(
        _router_kernel,
        grid=(n // ROUTER_TM,),
        in_specs=[pl.BlockSpec((ROUTER_TM, D_MODEL), lambda i: (i, 0)),
                  pl.BlockSpec((N_EXPERTS, D_MODEL), lambda i: (0, 0)),
                  pl.BlockSpec((N_EXPERTS, 1), lambda i: (0, 0)),
                  pl.BlockSpec((ROUTER_TM, ROUTER_TM), lambda i: (0, 0))],
        out_specs=[out, out,
                   pl.BlockSpec((CLS_ROWS, LANES), lambda i: (0, 0)),
                   pl.BlockSpec((ROUTER_TM, LANES), lambda i: (i, 0))],
        out_shape=[jax.ShapeDtypeStruct((1, n), jnp.int32),
                   jax.ShapeDtypeStruct((1, n), jnp.int32),
                   jax.ShapeDtypeStruct((CLS_ROWS, LANES), _F32),
                   jax.ShapeDtypeStruct((n, LANES), _F32)],
        scratch_shapes=[pltpu.VMEM((CLS_ROWS, ROUTER_TM), _F32)],
        compiler_params=_params("arbitrary"),
        name="router",
    )(h, w_t, b_t, tri)


def _class_experts():
    lo, hi = [], []
    for g in range(N_EXPERT_GROUPS):
        for a in range(EXPERTS_PER_GROUP):
            for b in range(a + 1, EXPERTS_PER_GROUP):
                lo.append(g * EXPERTS_PER_GROUP + a)
                hi.append(g * EXPERTS_PER_GROUP + b)
    return jnp.array(lo, jnp.int32), jnp.array(hi, jnp.int32)


def _expert_kernel(elo_ref, ehi_ref, nused_ref, x_ref, wg1, wu1, wd1, wg2, wu2, wd2, o_ref):
    t = pl.program_id(0)

    @pl.when(t < nused_ref[0])
    def _():
        x = x_ref[:, :D_MODEL].astype(_BF16)
        gates = x_ref[:, D_MODEL:]

        def ffn(wg, wu, wd, gate):
            a = jnp.dot(x, wg[0], preferred_element_type=_F32)
            u = jnp.dot(x, wu[0], preferred_element_type=_F32)
            hid = (a * _sigmoid(a)) * u * gate
            return jnp.dot(hid.astype(_BF16), wd[0], preferred_element_type=_F32)

        o_ref[...] = ffn(wg1, wu1, wd1, gates[:, 0:1]) + ffn(wg2, wu2, wd2, gates[:, 1:2])

    @pl.when(t >= nused_ref[0])
    def _():
        o_ref[...] = jnp.zeros_like(o_ref)


def _experts(x_sorted, tile_lo, tile_hi, n_used, w_gate, w_up, w_down):
    up_spec = lambda which: pl.BlockSpec(
        (1, D_MODEL, D_EXPERT), lambda t, lo, hi, nu: ((lo, hi)[which][t], 0, 0))
    down_spec = lambda which: pl.BlockSpec(
        (1, D_EXPERT, D_MODEL), lambda t, lo, hi, nu: ((lo, hi)[which][t], 0, 0))
    row = lambda d: pl.BlockSpec((MOE_TM, d), lambda t, lo, hi, nu: (t, 0))
    grid_spec = pltpu.PrefetchScalarGridSpec(
        num_scalar_prefetch=3,
        grid=(MOE_TILES,),
        in_specs=[row(XROW), up_spec(0), up_spec(0), down_spec(0),
                  up_spec(1), up_spec(1), down_spec(1)],
        out_specs=row(D_MODEL),
    )
    return pl.pallas_call(
        _expert_kernel,
        grid_spec=grid_spec,
        out_shape=jax.ShapeDtypeStruct((MOE_ROWS, D_MODEL), _F32),
        compiler_params=_params("arbitrary"),
        name="experts",
    )(tile_lo, tile_hi, n_used, x_sorted, w_gate, w_up, w_down, w_gate, w_up, w_down)


def _wait_rows(copy, n):
    def body(r, carry):
        copy.wait()
        return carry
    lax.fori_loop(0, n, body, 0)


def _dispatch_kernel(pos_ref, h_ref, g_ref, init_ref, out_ref, stage_ref, sem):
    del init_ref
    s = pl.program_id(0)
    last = pl.num_programs(0) - 1
    slot = s % 2

    def row_copy(slot, r, p):
        return pltpu.make_async_copy(stage_ref.at[slot, pl.ds(r, 1), :],
                                     out_ref.at[pl.ds(p, 1), :], sem.at[slot])

    @pl.when(s >= 2)
    def _():
        _wait_rows(row_copy(slot, 0, 0), ROW_TM)

    stage_ref[slot, :, :D_MODEL] = h_ref[...]
    stage_ref[slot, :, D_MODEL:] = g_ref[...]

    def issue(r, carry):
        row_copy(slot, r, pos_ref[s * ROW_TM + r]).start()
        return carry

    lax.fori_loop(0, ROW_TM, issue, 0, unroll=8)

    @pl.when(s == last)
    def _():
        _wait_rows(row_copy(slot, 0, 0), ROW_TM)
        _wait_rows(row_copy(1 - slot, 0, 0), ROW_TM)


def _dispatch(pos, h, gext):
    n = h.shape[0]
    assert n // ROW_TM >= 2
    grid_spec = pltpu.PrefetchScalarGridSpec(
        num_scalar_prefetch=1,
        grid=(n // ROW_TM,),
        in_specs=[pl.BlockSpec((ROW_TM, D_MODEL), lambda i, pos: (i, 0)),
                  pl.BlockSpec((ROW_TM, LANES), lambda i, pos: (i, 0)),
                  pl.BlockSpec(memory_space=pl.ANY)],
        out_specs=pl.BlockSpec(memory_space=pl.ANY),
        scratch_shapes=[pltpu.VMEM((2, ROW_TM, XROW), _F32), pltpu.SemaphoreType.DMA((2,))],
    )
    return pl.pallas_call(
        _dispatch_kernel,
        grid_spec=grid_spec,
        out_shape=jax.ShapeDtypeStruct((MOE_ROWS, XROW), _F32),
        input_output_aliases={3: 0},
        compiler_params=_params("arbitrary"),
        name="dispatch",
    )(pos, h, gext, jnp.zeros((MOE_ROWS, XROW), _F32))


def _combine_kernel(pos_ref, h_ref, y_ref, g_ref, b_ref, o_ref, buf_ref, sem):
    s = pl.program_id(0)
    last = pl.num_programs(0) - 1
    slot = s % 2

    def row_copy(slot, r, p):
        return pltpu.make_async_copy(y_ref.at[pl.ds(p, 1), :],
                                     buf_ref.at[slot, pl.ds(r, 1), :], sem.at[slot])

    def issue_tile(step, slot):
        def issue(r, carry):
            row_copy(slot, r, pos_ref[step * ROW_TM + r]).start()
            return carry
        lax.fori_loop(0, ROW_TM, issue, 0, unroll=8)

    @pl.when(s == 0)
    def _():
        issue_tile(0, 0)

    @pl.when(s < last)
    def _():
        issue_tile(s + 1, 1 - slot)

    _wait_rows(row_copy(slot, 0, 0), ROW_TM)
    o_ref[...] = _layer_norm(ALPHA * h_ref[...] + buf_ref[slot], g_ref[...], b_ref[...])


def _combine(pos, h, y_sorted, g, b):
    n = h.shape[0]
    row = pl.BlockSpec((ROW_TM, D_MODEL), lambda i, pos: (i, 0))
    vec = pl.BlockSpec((1, D_MODEL), lambda i, pos: (0, 0))
    grid_spec = pltpu.PrefetchScalarGridSpec(
        num_scalar_prefetch=1,
        grid=(n // ROW_TM,),
        in_specs=[row, pl.BlockSpec(memory_space=pl.ANY), vec, vec],
        out_specs=row,
        scratch_shapes=[pltpu.VMEM((2, ROW_TM, D_MODEL), _F32), pltpu.SemaphoreType.DMA((2,))],
    )
    return pl.pallas_call(
        _combine_kernel,
        grid_spec=grid_spec,
        out_shape=jax.ShapeDtypeStruct((n, D_MODEL), _F32),
        compiler_params=_params("arbitrary"),
        name="combine",
    )(pos, h, y_sorted, g.reshape(1, D_MODEL), b.reshape(1, D_MODEL))


def _moe_layer(h, router_w, router_b, w_gate, w_up, w_down, ln_g, ln_b):
    cls, rank, cnt, gext = _router(h, router_w, router_b)
    counts = cnt[:N_CLASSES, 0].astype(jnp.int32)
    tiles = (counts + MOE_TM - 1) // MOE_TM
    tile_end = jnp.cumsum(tiles)
    class_start = (tile_end - tiles) * MOE_TM
    classes = jnp.arange(N_CLASSES, dtype=jnp.int32)
    pos = jnp.sum(jnp.where(cls[0][:, None] == classes[None, :], class_start[None, :], 0),
                  axis=1) + rank[0]
    n_used = tile_end[-1]
    tile_ids = jnp.minimum(jnp.arange(MOE_TILES, dtype=jnp.int32), n_used - 1)
    tile_cls = jnp.sum((tile_ids[:, None] >= tile_end[None, :]).astype(jnp.int32), axis=1)
    class_lo, class_hi = _class_experts()
    x_sorted = _dispatch(pos, h, gext)
    y_sorted = _experts(x_sorted, class_lo[tile_cls], class_hi[tile_cls],
                        n_used.reshape(1).astype(jnp.int32), w_gate, w_up, w_down)
    return _combine(pos, h, y_sorted, ln_g, ln_b)


def kernel(x, meta_tokens, mamba_w_in, mamba_conv_w, mamba_conv_b, mamba_dt_bias, mamba_a_log,
           mamba_d_skip, mamba_norm_w, mamba_w_out, sb_w_q, sb_w_o, shared_w_k, shared_w_v,
           ln_mix_g, ln_mix_b, ln_ffn_g, ln_ffn_b, router_w, router_b, moe_w_gate, moe_w_up,
           moe_w_down):
    bsz = x.shape[0]
    meta = jnp.broadcast_to(meta_tokens.astype(x.dtype)[None], (bsz, N_META, D_MODEL))
    h = jnp.concatenate([jnp.zeros((bsz, PAD, D_MODEL), x.dtype), meta, x], axis=1)
    h = h.reshape(NP, D_MODEL)
    zx_dim = D_INNER + CONV_DIM
    kv = None
    for layer in range(DEPTH):
        if layer < N_A_LAYERS:
            w_in = mamba_w_in[layer].astype(_BF16)
            w_dt = jnp.pad(w_in[:, zx_dim:], ((0, 0), (0, LANES - SSM_HEADS)))
            proj = _matmul(h, w_in, zx_dim, 512, 1024, _BF16)
            dt_raw = _matmul(h, w_dt, LANES, 512, LANES, _F32)
            xbc = _conv_silu(proj.reshape(BATCH, LP, zx_dim), mamba_conv_w[layer].astype(_F32),
                             mamba_conv_b[layer].astype(_F32))
            y = _ssd(xbc, dt_raw.reshape(BATCH, LP, LANES), mamba_dt_bias[layer],
                     mamba_a_log[layer], mamba_d_skip[layer])
            h = _mamba_out(y.reshape(NP, D_INNER), proj, mamba_norm_w[layer],
                           mamba_w_out[layer].astype(_BF16), h, ln_mix_g[layer], ln_mix_b[layer])
        else:
            j = layer - N_A_LAYERS
            if kv is None:
                w_kv = jnp.concatenate([shared_w_k, shared_w_v], axis=1).astype(_BF16)
                kv = _matmul(h, w_kv, 2 * SB_WIDTH, 512, 1024, _BF16).reshape(BATCH, LP, 2 * SB_WIDTH)
                kv = jnp.pad(kv[:, PAD:], ((0, 0), (PAD + KV_FRONT, 0), (0, 0)))
            q = _matmul(h, sb_w_q[j].astype(_BF16), SB_WIDTH, 512, 1024, _BF16,
                        scale=SB_SCALE * LOG2_E)
            o = _stick_breaking(q.reshape(BATCH, LP, SB_WIDTH), kv)
            h = _attn_out(o.reshape(NP, SB_WIDTH), sb_w_o[j].astype(_BF16), h,
                          ln_mix_g[layer], ln_mix_b[layer])
        h = _moe_layer(h, router_w, router_b, moe_w_gate[layer].astype(_BF16),
                       moe_w_up[layer].astype(_BF16), moe_w_down[layer].astype(_BF16),
                       ln_ffn_g[layer], ln_ffn_b[layer])
    return h.reshape(bsz, LP, D_MODEL)[:, PAD + N_META:]
```

```python
import functools
import math

import jax
import jax.numpy as jnp
from jax import lax
from jax.experimental import pallas as pl
from jax.experimental.pallas import tpu as pltpu

D_MODEL = 1024
BATCH = 8
SEQ = 2048
DEPTH = 4
N_META = 16
N_A_LAYERS = DEPTH // 2
ALPHA = (2.0 * DEPTH) ** 0.25
LN_EPS = 1e-5
D_INNER = 2048
SSM_HEAD_DIM = 64
SSM_HEADS = 32
SSM_GROUPS = 4
D_STATE = 128
CONV_WIDTH = 4
CHUNK = 128
CONV_DIM = D_INNER + 2 * SSM_GROUPS * D_STATE
SB_HEADS = 16
SB_HEAD_DIM = 64
SB_WIDTH = SB_HEADS * SB_HEAD_DIM
SB_SCALE = SB_HEAD_DIM ** -0.5
LOG2_E = math.log2(math.e)
N_EXPERTS = 16
N_EXPERT_GROUPS = 4
EXPERTS_PER_GROUP = 4
D_EXPERT = 512

L_REAL = N_META + SEQ
PAD = (-L_REAL) % CHUNK
LP = L_REAL + PAD
N_CHUNKS = LP // CHUNK
NP = BATCH * LP
LANES = 128
N_PAIR_CLASSES = 6
N_CLASSES = N_EXPERT_GROUPS * N_PAIR_CLASSES
MOE_TM = 256
MOE_TILES = NP // MOE_TM + N_CLASSES
MOE_ROWS = MOE_TILES * MOE_TM
CLS_ROWS = 32
XROW = D_MODEL + LANES
ROW_TM = 256
ROW_UNROLL = 8
VMEM_LIMIT = 48 * 1024 * 1024

_F32 = jnp.float32
_BF16 = jnp.bfloat16


def _params(*sem):
    return pltpu.CompilerParams(dimension_semantics=sem, vmem_limit_bytes=VMEM_LIMIT)


def _sigmoid(x):
    return 1.0 / (1.0 + jnp.exp(-x))


def _softplus(x):
    return jnp.maximum(x, 0.0) + jnp.log(1.0 + jnp.exp(-jnp.abs(x)))


def _layer_norm(t, g, b):
    mu = jnp.mean(t, axis=-1, keepdims=True)
    d = t - mu
    var = jnp.mean(d * d, axis=-1, keepdims=True)
    return d * lax.rsqrt(var + LN_EPS) * g + b


def _mm_kernel(x_ref, w_ref, o_ref, *, scale):
    x = x_ref[...].astype(_BF16)
    acc = jnp.dot(x, w_ref[...], preferred_element_type=_F32)
    if scale != 1.0:
        acc = acc * scale
    o_ref[...] = acc.astype(o_ref.dtype)


def _matmul(x, w, n_out, tm, tn, out_dtype, scale=1.0):
    m, k = x.shape
    return pl.pallas_call(
        functools.partial(_mm_kernel, scale=scale),
        grid=(n_out // tn, m // tm),
        in_specs=[pl.BlockSpec((tm, k), lambda j, i: (i, 0)),
                  pl.BlockSpec((k, tn), lambda j, i: (0, j))],
        out_specs=pl.BlockSpec((tm, tn), lambda j, i: (i, j)),
        out_shape=jax.ShapeDtypeStruct((m, n_out), out_dtype),
        compiler_params=_params("arbitrary", "arbitrary"),
        name="matmul",
    )(x, w)


def _add_ln_kernel(h_ref, m_ref, g_ref, b_ref, o_ref):
    t = ALPHA * h_ref[...] + m_ref[...]
    o_ref[...] = _layer_norm(t, g_ref[...], b_ref[...])


def _add_ln(h, mix, g, b, tm=512):
    n, d = h.shape
    row = pl.BlockSpec((tm, d), lambda i: (i, 0))
    vec = pl.BlockSpec((1, d), lambda i: (0, 0))
    return pl.pallas_call(
        _add_ln_kernel,
        grid=(n // tm,),
        in_specs=[row, row, vec, vec],
        out_specs=row,
        out_shape=jax.ShapeDtypeStruct((n, d), _F32),
        compiler_params=_params("arbitrary"),
        name="add_ln",
    )(h, mix, g.reshape(1, d), b.reshape(1, d))


CONV_TN = 512
CONV_HALO = 16


def _conv_kernel(x_ref, w_ref, b_ref, o_ref):
    w = w_ref[...]
    bias = b_ref[...]
    for c in range(N_CHUNKS):
        s = c * CHUNK
        if c == 0:
            cur = x_ref[0, 0:CHUNK, :].astype(_F32)
            rows = lax.broadcasted_iota(jnp.int32, cur.shape, 0)
            cur = jnp.where(rows >= PAD, cur, 0.0)
            xx = jnp.concatenate([jnp.zeros((CONV_HALO, CONV_TN), _F32), cur], axis=0)
        else:
            xx = x_ref[0, s - CONV_HALO:s + CHUNK, :].astype(_F32)
            if s - CONV_HALO < PAD:
                rows = lax.broadcasted_iota(jnp.int32, xx.shape, 0) + (s - CONV_HALO)
                xx = jnp.where(rows >= PAD, xx, 0.0)
        acc = bias + w[CONV_WIDTH - 1:CONV_WIDTH, :] * xx[CONV_HALO:, :]
        for back in range(1, CONV_WIDTH):
            shifted = pltpu.roll(xx, back, 0)
            k = CONV_WIDTH - 1 - back
            acc = acc + w[k:k + 1, :] * shifted[CONV_HALO:, :]
        o_ref[0, s:s + CHUNK, :] = (acc * _sigmoid(acc)).astype(o_ref.dtype)


def _conv_silu(proj, conv_w, conv_b):
    first = D_INNER // CONV_TN
    return pl.pallas_call(
        _conv_kernel,
        grid=(BATCH, CONV_DIM // CONV_TN),
        in_specs=[pl.BlockSpec((1, LP, CONV_TN), lambda b, j: (b, 0, first + j)),
                  pl.BlockSpec((CONV_WIDTH, CONV_TN), lambda b, j: (0, j)),
                  pl.BlockSpec((1, CONV_TN), lambda b, j: (0, j))],
        out_specs=pl.BlockSpec((1, LP, CONV_TN), lambda b, j: (b, 0, j)),
        out_shape=jax.ShapeDtypeStruct((BATCH, LP, CONV_DIM), _BF16),
        compiler_params=_params("arbitrary", "arbitrary"),
        name="conv_silu",
    )(proj, conv_w, conv_b.reshape(1, CONV_DIM))


N_HEAD_PAIRS = SSM_HEADS // 2
PAIRS_PER_GROUP = N_HEAD_PAIRS // SSM_GROUPS


def _ssd_kernel(xs_ref, b_ref, c_ref, dtr_ref, dtb_ref, aneg_ref, dskip_ref, ltri_ref, exp_ref,
                y_ref, state_ref):
    c = pl.program_id(1)

    @pl.when(c == 0)
    def _():
        state_ref[...] = jnp.zeros_like(state_ref)

    rows = lax.broadcasted_iota(jnp.int32, (CHUNK, LANES), 0)
    cols = lax.broadcasted_iota(jnp.int32, (CHUNK, LANES), 1)
    causal = cols <= rows
    left = cols < SSM_HEAD_DIM

    dt = _softplus(dtr_ref[0] + dtb_ref[...])
    dt = jnp.where(rows + c * CHUNK >= PAD, dt, 0.0)
    a = dt * aneg_ref[...]
    a_cum = jnp.dot(ltri_ref[...], a, preferred_element_type=_F32,
                    precision=lax.Precision.HIGHEST)
    a_cum_t = a_cum.T
    total = a_cum[CHUNK - 1:CHUNK, :]
    from_start = jnp.exp(a_cum)
    to_end = jnp.exp(total - a_cum)
    chunk_decay = jnp.exp(jnp.broadcast_to(total, (8, LANES)))

    expand = exp_ref[...]
    dt_x = jnp.dot(dt.astype(_BF16), expand, preferred_element_type=_F32)
    to_end_x = jnp.dot(to_end.astype(_BF16), expand, preferred_element_type=_F32)
    chunk_decay_x = jnp.dot(chunk_decay.astype(_BF16), expand, preferred_element_type=_F32)

    for g in range(SSM_GROUPS):
        bg = b_ref[0, :, g * D_STATE:(g + 1) * D_STATE]
        cg = c_ref[0, :, g * D_STATE:(g + 1) * D_STATE]
        cb = lax.dot_general(cg, bg, (((1,), (1,)), ((), ())), preferred_element_type=_F32)
        bg_t = bg.astype(_F32).T.astype(_BF16)
        cg32 = cg.astype(_F32)
        for jj in range(PAIRS_PER_GROUP):
            j = g * PAIRS_PER_GROUP + jj
            sl = slice(j * LANES, (j + 1) * LANES)
            x2 = xs_ref[0, :, sl].astype(_F32)
            xdt = x2 * dt_x[:, sl]
            parts = []
            for h in (2 * j, 2 * j + 1):
                seg = a_cum[:, h:h + 1] - a_cum_t[h:h + 1, :]
                decay = jnp.where(causal, jnp.exp(seg), 0.0)
                parts.append((cb * decay).astype(_BF16))
            for h in (2 * j, 2 * j + 1):
                parts.append((cg32 * from_start[:, h:h + 1]).astype(_BF16))
            lhs = jnp.concatenate(parts, axis=1)
            state = state_ref[j]
            xb = xdt.astype(_BF16)
            sb = state.astype(_BF16)
            zero = jnp.zeros_like(xb)
            rhs = jnp.concatenate([jnp.where(left, xb, zero), jnp.where(left, zero, xb),
                                   jnp.where(left, sb, zero), jnp.where(left, zero, sb)], axis=0)
            y2 = jnp.dot(lhs, rhs, preferred_element_type=_F32)
            y2 = y2 + dskip_ref[:, sl] * x2
            y_ref[0, :, sl] = y2.astype(y_ref.dtype)
            upd = jnp.dot(bg_t, (xdt * to_end_x[:, sl]).astype(_BF16), preferred_element_type=_F32)
            state_ref[j] = state * chunk_decay_x[0:1, sl] + upd


def _ssd(xbc, dt_raw, dt_bias, a_log, d_skip):
    pad_h = LANES - SSM_HEADS
    dtb = jnp.pad(dt_bias.astype(_F32), (0, pad_h)).reshape(1, LANES)
    aneg = jnp.pad(-jnp.exp(a_log.astype(_F32)), (0, pad_h)).reshape(1, LANES)
    dskip = jnp.repeat(d_skip.astype(_F32), SSM_HEAD_DIM).reshape(1, D_INNER)
    ltri = jnp.tril(jnp.ones((CHUNK, CHUNK), _F32))
    expand = (jnp.arange(LANES)[:, None] == (jnp.arange(D_INNER)[None, :] // SSM_HEAD_DIM)).astype(_BF16)
    gn = SSM_GROUPS * D_STATE
    vec = lambda n: pl.BlockSpec((1, n), lambda b, c: (0, 0))
    return pl.pallas_call(
        _ssd_kernel,
        grid=(BATCH, N_CHUNKS),
        in_specs=[pl.BlockSpec((1, CHUNK, D_INNER), lambda b, c: (b, c, 0)),
                  pl.BlockSpec((1, CHUNK, gn), lambda b, c: (b, c, D_INNER // gn)),
                  pl.BlockSpec((1, CHUNK, gn), lambda b, c: (b, c, D_INNER // gn + 1)),
                  pl.BlockSpec((1, CHUNK, LANES), lambda b, c: (b, c, 0)),
                  vec(LANES), vec(LANES), vec(D_INNER),
                  pl.BlockSpec((CHUNK, CHUNK), lambda b, c: (0, 0)),
                  pl.BlockSpec((LANES, D_INNER), lambda b, c: (0, 0))],
        out_specs=pl.BlockSpec((1, CHUNK, D_INNER), lambda b, c: (b, c, 0)),
        out_shape=jax.ShapeDtypeStruct((BATCH, LP, D_INNER), _BF16),
        scratch_shapes=[pltpu.VMEM((N_HEAD_PAIRS, D_STATE, LANES), _F32)],
        compiler_params=_params("arbitrary", "arbitrary"),
        name="ssd",
    )(xbc, xbc, xbc, dt_raw, dtb, aneg, dskip, ltri, expand)


def _mamba_out_kernel(y_ref, z_ref, nw_ref, w_ref, h_ref, g_ref, b_ref, o_ref):
    y = y_ref[...].astype(_F32)
    z = z_ref[...].astype(_F32)
    yg = y * (z * _sigmoid(z))
    gw = D_INNER // SSM_GROUPS
    parts = []
    for g in range(SSM_GROUPS):
        blk = yg[:, g * gw:(g + 1) * gw]
        ms = jnp.mean(blk * blk, axis=-1, keepdims=True)
        parts.append(blk * lax.rsqrt(ms + LN_EPS))
    yn = jnp.concatenate(parts, axis=1) * nw_ref[...]
    mix = jnp.dot(yn.astype(_BF16), w_ref[...], preferred_element_type=_F32)
    o_ref[...] = _layer_norm(ALPHA * h_ref[...] + mix, g_ref[...], b_ref[...])


def _mamba_out(y, proj, norm_w, w_out, h, g, b, tm=256):
    n = h.shape[0]
    vec = lambda d: pl.BlockSpec((1, d), lambda i: (0, 0))
    return pl.pallas_call(
        _mamba_out_kernel,
        grid=(n // tm,),
        in_specs=[pl.BlockSpec((tm, D_INNER), lambda i: (i, 0)),
                  pl.BlockSpec((tm, D_INNER), lambda i: (i, 0)),
                  vec(D_INNER),
                  pl.BlockSpec((D_INNER, D_MODEL), lambda i: (0, 0)),
                  pl.BlockSpec((tm, D_MODEL), lambda i: (i, 0)),
                  vec(D_MODEL), vec(D_MODEL)],
        out_specs=pl.BlockSpec((tm, D_MODEL), lambda i: (i, 0)),
        out_shape=jax.ShapeDtypeStruct((n, D_MODEL), _F32),
        compiler_params=_params("arbitrary"),
        name="mamba_out",
    )(y, proj, norm_w.reshape(1, D_INNER).astype(_F32), w_out, h,
      g.reshape(1, D_MODEL), b.reshape(1, D_MODEL))


def _attn_out_kernel(o_ref_in, w_ref, h_ref, g_ref, b_ref, o_ref):
    mix = jnp.dot(o_ref_in[...], w_ref[...], preferred_element_type=_F32)
    o_ref[...] = _layer_norm(ALPHA * h_ref[...] + mix, g_ref[...], b_ref[...])


def _attn_out(o, w_o, h, g, b, tm=512):
    n = h.shape[0]
    vec = pl.BlockSpec((1, D_MODEL), lambda i: (0, 0))
    return pl.pallas_call(
        _attn_out_kernel,
        grid=(n // tm,),
        in_specs=[pl.BlockSpec((tm, SB_WIDTH), lambda i: (i, 0)),
                  pl.BlockSpec((SB_WIDTH, D_MODEL), lambda i: (0, 0)),
                  pl.BlockSpec((tm, D_MODEL), lambda i: (i, 0)),
                  vec, vec],
        out_specs=pl.BlockSpec((tm, D_MODEL), lambda i: (i, 0)),
        out_shape=jax.ShapeDtypeStruct((n, D_MODEL), _F32),
        compiler_params=_params("arbitrary"),
        name="attn_out",
    )(o, w_o, h, g.reshape(1, D_MODEL), b.reshape(1, D_MODEL))


SB_TQ = 128
SB_TK = 256
KV_FRONT = SB_TK - SB_TQ
LKV = LP + KV_FRONT


SB_HPS = 4
SB_LW = SB_HPS * SB_HEAD_DIM
N_QT = LP // SB_TQ
MASKED_SCORE = -1e30
SB_DEAD_LOG2 = 150.0


def _sb_kernel(q_ref, k_ref, v_ref, tri_ref, o_ref,
               qm_ref, vm_ref, lb_ref, sp_ref, w_ref, rs_ref, tot_ref):
    tri = tri_ref[...]
    col_minus_row = (lax.broadcasted_iota(jnp.int32, (SB_TQ, SB_TK), 1)
                     - lax.broadcasted_iota(jnp.int32, (SB_TQ, SB_TK), 0))

    def mask_q(c, carry):
        rows = pl.ds(pl.multiple_of(c * SB_TQ, SB_TQ), SB_TQ)
        blk = q_ref[0, rows, :]
        head = lax.broadcasted_iota(jnp.int32, blk.shape, 1) // SB_HEAD_DIM
        for h in range(SB_HPS):
            qm_ref[h, rows, :] = jnp.where(head == h, blk, jnp.zeros_like(blk))
        return carry

    def mask_v(c, carry):
        rows = pl.ds(pl.multiple_of(c * SB_TK, SB_TK), SB_TK)
        blk = v_ref[0, rows, :]
        head = lax.broadcasted_iota(jnp.int32, blk.shape, 1) // SB_HEAD_DIM
        for h in range(SB_HPS):
            vm_ref[h, rows, :] = jnp.where(head == h, blk, jnp.zeros_like(blk))
        return carry

    lax.fori_loop(0, N_QT, mask_q, 0)
    lax.fori_loop(0, LKV // SB_TK, mask_v, 0)

    def stage_a(item, slot):
        qi, kb = item[0], item[1]
        q_rows = pl.ds(pl.multiple_of(qi * SB_TQ, SB_TQ), SB_TQ)
        kblk = k_ref[0, pl.ds(pl.multiple_of(kb * SB_TK, SB_TK), SB_TK), :]
        valid = col_minus_row < qi * SB_TQ + KV_FRONT - kb * SB_TK
        for h in range(SB_HPS):
            z = lax.dot_general(qm_ref[h, q_rows, :], kblk, (((1,), (1,)), ((), ())),
                                preferred_element_type=_F32)
            z = jnp.where(valid, z, MASKED_SCORE)
            neg_abs = pltpu.bitcast(pltpu.bitcast(z, jnp.uint32) | jnp.uint32(0x80000000), _F32)
            s = jnp.maximum(z, 0.0) + jnp.log2(1.0 + jnp.exp2(neg_abs))
            lb_ref[slot, h] = z - s
            sp_ref[slot, h] = s.astype(_BF16)
            rs_ref[slot, h] = jnp.broadcast_to(jnp.sum(s, axis=-1, keepdims=True), (SB_TQ, LANES))

    def stage_b(slot):
        for h in range(SB_HPS):
            later = jnp.dot(sp_ref[slot, h], tri, preferred_element_type=_F32)
            total = jnp.tile(tot_ref[h], (1, SB_TK // LANES))
            w_ref[slot, h] = jnp.exp2(lb_ref[slot, h] - later - total).astype(_BF16)

    def next_totals(item, prev_slot):
        fresh = item[2] == 1
        low = None
        for h in range(SB_HPS):
            tot = jnp.where(fresh, 0.0, tot_ref[h] + rs_ref[prev_slot, h])
            tot_ref[h] = tot
            low = tot if low is None else jnp.minimum(low, tot)
        return jnp.min(low) >= SB_DEAD_LOG2

    def stage_c(item, slot, acc):
        qi, kb, first, live = item
        k_rows = pl.ds(pl.multiple_of(kb * SB_TK, SB_TK), SB_TK)
        new = jnp.where(first == 1, 0.0, acc)
        for h in range(SB_HPS):
            new = new + jnp.dot(w_ref[slot, h], vm_ref[h, k_rows, :], preferred_element_type=_F32)
        acc = jnp.where(live == 1, new, acc)
        o_ref[0, pl.ds(pl.multiple_of(qi * SB_TQ, SB_TQ), SB_TQ), :] = acc.astype(o_ref.dtype)
        return acc

    def key_blocks(qi):
        return (qi * SB_TQ + SB_TQ + KV_FRONT + SB_TK - 1) // SB_TK

    def advance(item, dead):
        qi, kb, _, live = item
        tile_done = jnp.logical_or(kb == 0, dead)
        nxt = jnp.where(tile_done, qi + 1, qi)
        live = jnp.where(nxt >= N_QT, 0, live)
        nxt = jnp.minimum(nxt, N_QT - 1)
        kb = jnp.where(tile_done, key_blocks(nxt) - 1, kb - 1)
        return nxt, kb, tile_done.astype(jnp.int32), live

    def step(items, slot, acc):
        cur, prev, prev2 = items
        acc = stage_c(prev2, slot, acc)
        stage_b(1 - slot)
        dead = next_totals(cur, 1 - slot)
        stage_a(cur, slot)
        return (advance(cur, dead), cur, prev), acc

    def two_steps(carry):
        items, acc = carry
        items, acc = step(items, 0, acc)
        return step(items, 1, acc)

    def unfinished(carry):
        (cur, prev, prev2), _ = carry
        return cur[3] + prev[3] + prev2[3] > 0

    as_item = lambda *v: tuple(jnp.int32(x) for x in v)
    item0 = as_item(0, key_blocks(0) - 1, 1, 1)
    item1 = as_item(1, key_blocks(1) - 1, 1, 1)
    tot_ref[...] = jnp.zeros_like(tot_ref)
    stage_a(item0, 0)
    stage_b(0)
    dead = next_totals(item1, 0)
    stage_a(item1, 1)
    carry = ((advance(item1, dead), item1, item0), jnp.zeros((SB_TQ, SB_LW), _F32))
    lax.while_loop(unfinished, two_steps, carry)


def _stick_breaking(q, kv):
    j = jnp.arange(SB_TK)
    tri = (j[:, None] > j[None, :]).astype(_BF16)
    n_steps = SB_WIDTH // SB_LW
    slots = lambda dtype: pltpu.VMEM((2, SB_HPS, SB_TQ, SB_TK), dtype)
    return pl.pallas_call(
        _sb_kernel,
        grid=(BATCH, n_steps),
        in_specs=[pl.BlockSpec((1, LP, SB_LW), lambda b, p: (b, 0, p)),
                  pl.BlockSpec((1, LKV, SB_LW), lambda b, p: (b, 0, p)),
                  pl.BlockSpec((1, LKV, SB_LW), lambda b, p: (b, 0, p + n_steps)),
                  pl.BlockSpec((SB_TK, SB_TK), lambda b, p: (0, 0))],
        out_specs=pl.BlockSpec((1, LP, SB_LW), lambda b, p: (b, 0, p)),
        scratch_shapes=[pltpu.VMEM((SB_HPS, LP, SB_LW), _BF16),
                        pltpu.VMEM((SB_HPS, LKV, SB_LW), _BF16),
                        slots(_F32), slots(_BF16), slots(_BF16),
                        pltpu.VMEM((2, SB_HPS, SB_TQ, LANES), _F32),
                        pltpu.VMEM((SB_HPS, SB_TQ, LANES), _F32)],
        out_shape=jax.ShapeDtypeStruct((BATCH, LP, SB_WIDTH), _BF16),
        compiler_params=_params("arbitrary", "arbitrary"),
        name="stick_breaking",
    )(q, kv, kv, tri)


ROUTER_TM = 512


def _router_kernel(h_ref, w_ref, b_ref, tri_ref, cls_ref, rank_ref, cnt_ref, gext_ref, base_ref):
    @pl.when(pl.program_id(0) == 0)
    def _():
        base_ref[...] = jnp.zeros_like(base_ref)

    logits = lax.dot_general(w_ref[...], h_ref[...], (((1,), (1,)), ((), ())),
                             preferred_element_type=_F32,
                             precision=lax.Precision.HIGHEST) + b_ref[...]
    m = jnp.max(logits, axis=0, keepdims=True)
    e = jnp.exp(logits - m)
    p = e / jnp.sum(e, axis=0, keepdims=True)
    ng = N_EXPERT_GROUPS
    pj = [p[j * ng:(j + 1) * ng, :] for j in range(EXPERTS_PER_GROUP)]
    m1 = jnp.maximum(jnp.maximum(pj[0], pj[1]), jnp.maximum(pj[2], pj[3]))
    i1 = jnp.where(pj[0] == m1, 0, jnp.where(pj[1] == m1, 1, jnp.where(pj[2] == m1, 2, 3)))
    qj = [jnp.where(i1 == j, -1.0, pj[j]) for j in range(EXPERTS_PER_GROUP)]
    m2 = jnp.maximum(jnp.maximum(qj[0], qj[1]), jnp.maximum(qj[2], qj[3]))
    i2 = jnp.where(qj[0] == m2, 0, jnp.where(qj[1] == m2, 1, jnp.where(qj[2] == m2, 2, 3)))
    score = m1 + m2
    gid = lax.broadcasted_iota(jnp.int32, score.shape, 0)
    best = jnp.max(score, axis=0, keepdims=True)
    gsel = jnp.min(jnp.where(score == best, gid, ng), axis=0, keepdims=True)
    chosen = gid == gsel
    pick_f = lambda v: jnp.sum(jnp.where(chosen, v, 0.0), axis=0, keepdims=True)
    pick_i = lambda v: jnp.sum(jnp.where(chosen, v, 0), axis=0, keepdims=True)
    v1, v2, j1, j2 = pick_f(m1), pick_f(m2), pick_i(i1), pick_i(i2)
    denom = v1 + v2
    g1 = v1 / denom
    g2 = v2 / denom
    first_low = j1 < j2
    lo = jnp.where(first_low, j1, j2)
    hi = jnp.where(first_low, j2, j1)
    base = jnp.where(lo == 0, 0, jnp.where(lo == 1, 3, 5))
    cls = gsel * N_PAIR_CLASSES + base + hi - lo - 1
    cls_ref[...] = cls

    member = lax.broadcasted_iota(jnp.int32, (CLS_ROWS, ROUTER_TM), 0) == cls
    onehot = jnp.where(member, 1.0, 0.0)
    upto = jnp.dot(onehot.astype(_BF16), tri_ref[...], preferred_element_type=_F32)
    before = base_ref[...]
    rank = jnp.sum(jnp.where(member, upto - 1.0 + before, 0.0), axis=0, keepdims=True)
    rank_ref[...] = rank.astype(jnp.int32)
    after = before + jnp.sum(onehot, axis=1, keepdims=True)
    base_ref[...] = after
    cnt_ref[...] = after[:, :LANES]

    gates = jnp.concatenate([jnp.where(first_low, g1, g2), jnp.where(first_low, g2, g1),
                             jnp.zeros((LANES - 2, ROUTER_TM), _F32)], axis=0)
    gext_ref[...] = gates.T


def _router(h, router_w, router_b):
    n = h.shape[0]
    perm = jnp.array([g * EXPERTS_PER_GROUP + j for j in range(EXPERTS_PER_GROUP)
                      for g in range(N_EXPERT_GROUPS)])
    w_t = router_w.astype(_F32).T[perm]
    b_t = router_b.astype(_F32)[perm].reshape(N_EXPERTS, 1)
    j = jnp.arange(ROUTER_TM)
    tri = (j[:, None] <= j[None, :]).astype(_BF16)
    out = pl.BlockSpec((1, ROUTER_TM), lambda i: (0, i))
    return pl.pallas_call(
        _router_kernel,
        grid=(n // ROUTER_TM,),
        in_specs=[pl.BlockSpec((ROUTER_TM, D_MODEL), lambda i: (i, 0)),
                  pl.BlockSpec((N_EXPERTS, D_MODEL), lambda i: (0, 0)),
                  pl.BlockSpec((N_EXPERTS, 1), lambda i: (0, 0)),
                  pl.BlockSpec((ROUTER_TM, ROUTER_TM), lambda i: (0, 0))],
        out_specs=[out, out,
                   pl.BlockSpec((CLS_ROWS, LANES), lambda i: (0, 0)),
                   pl.BlockSpec((ROUTER_TM, LANES), lambda i: (i, 0))],
        out_shape=[jax.ShapeDtypeStruct((1, n), jnp.int32),
                   jax.ShapeDtypeStruct((1, n), jnp.int32),
                   jax.ShapeDtypeStruct((CLS_ROWS, LANES), _F32),
                   jax.ShapeDtypeStruct((n, LANES), _F32)],
        scratch_shapes=[pltpu.VMEM((CLS_ROWS, ROUTER_TM), _F32)],
        compiler_params=_params("arbitrary"),
        name="router",
    )(h, w_t, b_t, tri)


def _class_experts():
    lo, hi = [], []
    for g in range(N_EXPERT_GROUPS):
        for a in range(EXPERTS_PER_GROUP):
            for b in range(a + 1, EXPERTS_PER_GROUP):
                lo.append(g * EXPERTS_PER_GROUP + a)
                hi.append(g * EXPERTS_PER_GROUP + b)
    return jnp.array(lo, jnp.int32), jnp.array(hi, jnp.int32)


def _expert_kernel(elo_ref, ehi_ref, nused_ref, x_ref, wg1, wu1, wd1, wg2, wu2, wd2, o_ref):
    t = pl.program_id(0)

    @pl.when(t < nused_ref[0])
    def _():
        x = x_ref[:, :D_MODEL].astype(_BF16)
        gates = x_ref[:, D_MODEL:]

        def ffn(wg, wu, wd, gate):
            a = jnp.dot(x, wg[0], preferred_element_type=_F32)
            u = jnp.dot(x, wu[0], preferred_element_type=_F32)
            hid = (a * _sigmoid(a)) * u * gate
            return jnp.dot(hid.astype(_BF16), wd[0], preferred_element_type=_F32)

        o_ref[...] = ffn(wg1, wu1, wd1, gates[:, 0:1]) + ffn(wg2, wu2, wd2, gates[:, 1:2])

    @pl.when(t >= nused_ref[0])
    def _():
        o_ref[...] = jnp.zeros_like(o_ref)


def _experts(x_sorted, tile_lo, tile_hi, n_used, w_gate, w_up, w_down):
    up_spec = lambda which: pl.BlockSpec(
        (1, D_MODEL, D_EXPERT), lambda t, lo, hi, nu: ((lo, hi)[which][t], 0, 0))
    down_spec = lambda which: pl.BlockSpec(
        (1, D_EXPERT, D_MODEL), lambda t, lo, hi, nu: ((lo, hi)[which][t], 0, 0))
    row = lambda d: pl.BlockSpec((MOE_TM, d), lambda t, lo, hi, nu: (t, 0))
    grid_spec = pltpu.PrefetchScalarGridSpec(
        num_scalar_prefetch=3,
        grid=(MOE_TILES,),
        in_specs=[row(XROW), up_spec(0), up_spec(0), down_spec(0),
                  up_spec(1), up_spec(1), down_spec(1)],
        out_specs=row(D_MODEL),
    )
    return pl.pallas_call(
        _expert_kernel,
        grid_spec=grid_spec,
        out_shape=jax.ShapeDtypeStruct((MOE_ROWS, D_MODEL), _F32),
        compiler_params=_params("arbitrary"),
        name="experts",
    )(tile_lo, tile_hi, n_used, x_sorted, w_gate, w_up, w_down, w_gate, w_up, w_down)


def _issue_rows(row_copy, pos_ref, base, slot):
    def issue(g, carry):
        for u in range(ROW_UNROLL):
            r = g * ROW_UNROLL + u
            row_copy(slot, r, pos_ref[base + r]).start(priority=u % 2)
        return carry
    lax.fori_loop(0, ROW_TM // ROW_UNROLL, issue, 0)


def _dispatch_kernel(pos_ref, h_ref, g_ref, init_ref, out_ref, stage_ref, sem):
    del init_ref
    s = pl.program_id(0)
    last = pl.num_programs(0) - 1
    slot = s % 2

    def row_copy(slot, r, p):
        return pltpu.make_async_copy(stage_ref.at[slot, pl.ds(r, 1), :],
                                     out_ref.at[pl.ds(p, 1), :], sem.at[slot])

    def wait_slot(slot):
        pltpu.make_async_copy(stage_ref.at[slot], out_ref.at[pl.ds(0, ROW_TM), :],
                              sem.at[slot]).wait()

    @pl.when(s >= 2)
    def _():
        wait_slot(slot)

    stage_ref[slot, :, :D_MODEL] = h_ref[...]
    stage_ref[slot, :, D_MODEL:] = g_ref[...]
    _issue_rows(row_copy, pos_ref, s * ROW_TM, slot)

    @pl.when(s == last)
    def _():
        wait_slot(slot)
        wait_slot(1 - slot)


def _dispatch(pos, h, gext):
    n = h.shape[0]
    assert n // ROW_TM >= 2
    grid_spec = pltpu.PrefetchScalarGridSpec(
        num_scalar_prefetch=1,
        grid=(n // ROW_TM,),
        in_specs=[pl.BlockSpec((ROW_TM, D_MODEL), lambda i, pos: (i, 0)),
                  pl.BlockSpec((ROW_TM, LANES), lambda i, pos: (i, 0)),
                  pl.BlockSpec(memory_space=pl.ANY)],
        out_specs=pl.BlockSpec(memory_space=pl.ANY),
        scratch_shapes=[pltpu.VMEM((2, ROW_TM, XROW), _F32), pltpu.SemaphoreType.DMA((2,))],
    )
    return pl.pallas_call(
        _dispatch_kernel,
        grid_spec=grid_spec,
        out_shape=jax.ShapeDtypeStruct((MOE_ROWS, XROW), _F32),
        input_output_aliases={3: 0},
        compiler_params=_params("arbitrary"),
        name="dispatch",
    )(pos, h, gext, jnp.zeros((MOE_ROWS, XROW), _F32))


def _combine_kernel(pos_ref, h_ref, y_ref, g_ref, b_ref, o_ref, buf_ref, sem):
    s = pl.program_id(0)
    last = pl.num_programs(0) - 1
    slot = s % 2

    def row_copy(slot, r, p):
        return pltpu.make_async_copy(y_ref.at[pl.ds(p, 1), :],
                                     buf_ref.at[slot, pl.ds(r, 1), :], sem.at[slot])

    @pl.when(s == 0)
    def _():
        _issue_rows(row_copy, pos_ref, 0, 0)

    @pl.when(s < last)
    def _():
        _issue_rows(row_copy, pos_ref, (s + 1) * ROW_TM, 1 - slot)

    pltpu.make_async_copy(y_ref.at[pl.ds(0, ROW_TM), :], buf_ref.at[slot], sem.at[slot]).wait()
    o_ref[...] = _layer_norm(ALPHA * h_ref[...] + buf_ref[slot], g_ref[...], b_ref[...])


def _combine(pos, h, y_sorted, g, b):
    n = h.shape[0]
    row = pl.BlockSpec((ROW_TM, D_MODEL), lambda i, pos: (i, 0))
    vec = pl.BlockSpec((1, D_MODEL), lambda i, pos: (0, 0))
    grid_spec = pltpu.PrefetchScalarGridSpec(
        num_scalar_prefetch=1,
        grid=(n // ROW_TM,),
        in_specs=[row, pl.BlockSpec(memory_space=pl.ANY), vec, vec],
        out_specs=row,
        scratch_shapes=[pltpu.VMEM((2, ROW_TM, D_MODEL), _F32), pltpu.SemaphoreType.DMA((2,))],
    )
    return pl.pallas_call(
        _combine_kernel,
        grid_spec=grid_spec,
        out_shape=jax.ShapeDtypeStruct((n, D_MODEL), _F32),
        compiler_params=_params("arbitrary"),
        name="combine",
    )(pos, h, y_sorted, g.reshape(1, D_MODEL), b.reshape(1, D_MODEL))


def _moe_layer(h, router_w, router_b, w_gate, w_up, w_down, ln_g, ln_b):
    cls, rank, cnt, gext = _router(h, router_w, router_b)
    counts = cnt[:N_CLASSES, 0].astype(jnp.int32)
    tiles = (counts + MOE_TM - 1) // MOE_TM
    tile_end = jnp.cumsum(tiles)
    class_start = (tile_end - tiles) * MOE_TM
    classes = jnp.arange(N_CLASSES, dtype=jnp.int32)
    pos = jnp.sum(jnp.where(cls[0][:, None] == classes[None, :], class_start[None, :], 0),
                  axis=1) + rank[0]
    n_used = tile_end[-1]
    tile_ids = jnp.minimum(jnp.arange(MOE_TILES, dtype=jnp.int32), n_used - 1)
    tile_cls = jnp.sum((tile_ids[:, None] >= tile_end[None, :]).astype(jnp.int32), axis=1)
    class_lo, class_hi = _class_experts()
    x_sorted = _dispatch(pos, h, gext)
    y_sorted = _experts(x_sorted, class_lo[tile_cls], class_hi[tile_cls],
                        n_used.reshape(1).astype(jnp.int32), w_gate, w_up, w_down)
    return _combine(pos, h, y_sorted, ln_g, ln_b)


def kernel(x, meta_tokens, mamba_w_in, mamba_conv_w, mamba_conv_b, mamba_dt_bias, mamba_a_log,
           mamba_d_skip, mamba_norm_w, mamba_w_out, sb_w_q, sb_w_o, shared_w_k, shared_w_v,
           ln_mix_g, ln_mix_b, ln_ffn_g, ln_ffn_b, router_w, router_b, moe_w_gate, moe_w_up,
           moe_w_down):
    bsz = x.shape[0]
    meta = jnp.broadcast_to(meta_tokens.astype(x.dtype)[None], (bsz, N_META, D_MODEL))
    h = jnp.concatenate([jnp.zeros((bsz, PAD, D_MODEL), x.dtype), meta, x], axis=1)
    h = h.reshape(NP, D_MODEL)
    zx_dim = D_INNER + CONV_DIM
    kv = None
    for layer in range(DEPTH):
        if layer < N_A_LAYERS:
            w_in = mamba_w_in[layer].astype(_BF16)
            w_dt = jnp.pad(w_in[:, zx_dim:], ((0, 0), (0, LANES - SSM_HEADS)))
            proj = _matmul(h, w_in, zx_dim, 512, zx_dim // 2, _BF16)
            dt_raw = _matmul(h, w_dt, LANES, 512, LANES, _F32)
            xbc = _conv_silu(proj.reshape(BATCH, LP, zx_dim), mamba_conv_w[layer].astype(_F32),
                             mamba_conv_b[layer].astype(_F32))
            y = _ssd(xbc, dt_raw.reshape(BATCH, LP, LANES), mamba_dt_bias[layer],
                     mamba_a_log[layer], mamba_d_skip[layer])
            h = _mamba_out(y.reshape(NP, D_INNER), proj, mamba_norm_w[layer],
                           mamba_w_out[layer].astype(_BF16), h, ln_mix_g[layer], ln_mix_b[layer])
        else:
            j = layer - N_A_LAYERS
            if kv is None:
                w_kv = jnp.concatenate([shared_w_k, shared_w_v], axis=1).astype(_BF16)
                kv = _matmul(h, w_kv, 2 * SB_WIDTH, 512, 2 * SB_WIDTH, _BF16).reshape(BATCH, LP, 2 * SB_WIDTH)
                kv = jnp.pad(kv[:, PAD:], ((0, 0), (PAD + KV_FRONT, 0), (0, 0)))
            q = _matmul(h, sb_w_q[j].astype(_BF16), SB_WIDTH, 512, 1024, _BF16,
                        scale=SB_SCALE * LOG2_E)
            o = _stick_breaking(q.reshape(BATCH, LP, SB_WIDTH), kv)
            h = _attn_out(o.reshape(NP, SB_WIDTH), sb_w_o[j].astype(_BF16), h,
                          ln_mix_g[layer], ln_mix_b[layer])
        h = _moe_layer(h, router_w, router_b, moe_w_gate[layer].astype(_BF16),
                       moe_w_up[layer].astype(_BF16), moe_w_down[layer].astype(_BF16),
                       ln_ffn_g[layer], ln_ffn_b[layer])
    return h.reshape(bsz, LP, D_MODEL)[:, PAD + N_META:]
```

```python
import functools
import math

import jax
import jax.numpy as jnp
from jax import lax
from jax.experimental import pallas as pl
from jax.experimental.pallas import tpu as pltpu

D_MODEL = 1024
BATCH = 8
SEQ = 2048
DEPTH = 4
N_META = 16
N_A_LAYERS = DEPTH // 2
ALPHA = (2.0 * DEPTH) ** 0.25
LN_EPS = 1e-5
D_INNER = 2048
SSM_HEAD_DIM = 64
SSM_HEADS = 32
SSM_GROUPS = 4
D_STATE = 128
CONV_WIDTH = 4
CHUNK = 128
CONV_DIM = D_INNER + 2 * SSM_GROUPS * D_STATE
SB_HEADS = 16
SB_HEAD_DIM = 64
SB_WIDTH = SB_HEADS * SB_HEAD_DIM
SB_SCALE = SB_HEAD_DIM ** -0.5
LOG2_E = math.log2(math.e)
N_EXPERTS = 16
N_EXPERT_GROUPS = 4
EXPERTS_PER_GROUP = 4
D_EXPERT = 512

L_REAL = N_META + SEQ
PAD = (-L_REAL) % CHUNK
LP = L_REAL + PAD
N_CHUNKS = LP // CHUNK
NP = BATCH * LP
LANES = 128
N_PAIR_CLASSES = 6
N_CLASSES = N_EXPERT_GROUPS * N_PAIR_CLASSES
MOE_TM = 256
MOE_TILES = NP // MOE_TM + N_CLASSES
MOE_ROWS = MOE_TILES * MOE_TM
CLS_ROWS = 32
XROW = D_MODEL + LANES
ROW_TM = 256
ROW_UNROLL = 8
VMEM_LIMIT = 48 * 1024 * 1024

_F32 = jnp.float32
_BF16 = jnp.bfloat16


def _params(*sem):
    return pltpu.CompilerParams(dimension_semantics=sem, vmem_limit_bytes=VMEM_LIMIT)


def _sigmoid(x):
    return 1.0 / (1.0 + jnp.exp(-x))


def _softplus(x):
    return jnp.maximum(x, 0.0) + jnp.log(1.0 + jnp.exp(-jnp.abs(x)))


def _layer_norm(t, g, b):
    mu = jnp.mean(t, axis=-1, keepdims=True)
    d = t - mu
    var = jnp.mean(d * d, axis=-1, keepdims=True)
    return d * lax.rsqrt(var + LN_EPS) * g + b


def _mm_kernel(x_ref, w_ref, o_ref, *, scale):
    x = x_ref[...].astype(_BF16)
    acc = jnp.dot(x, w_ref[...], preferred_element_type=_F32)
    if scale != 1.0:
        acc = acc * scale
    o_ref[...] = acc.astype(o_ref.dtype)


def _matmul(x, w, n_out, tm, tn, out_dtype, scale=1.0):
    m, k = x.shape
    return pl.pallas_call(
        functools.partial(_mm_kernel, scale=scale),
        grid=(n_out // tn, m // tm),
        in_specs=[pl.BlockSpec((tm, k), lambda j, i: (i, 0)),
                  pl.BlockSpec((k, tn), lambda j, i: (0, j))],
        out_specs=pl.BlockSpec((tm, tn), lambda j, i: (i, j)),
        out_shape=jax.ShapeDtypeStruct((m, n_out), out_dtype),
        compiler_params=_params("arbitrary", "arbitrary"),
        name="matmul",
    )(x, w)


def _add_ln_kernel(h_ref, m_ref, g_ref, b_ref, o_ref):
    t = ALPHA * h_ref[...] + m_ref[...]
    o_ref[...] = _layer_norm(t, g_ref[...], b_ref[...])


def _add_ln(h, mix, g, b, tm=512):
    n, d = h.shape
    row = pl.BlockSpec((tm, d), lambda i: (i, 0))
    vec = pl.BlockSpec((1, d), lambda i: (0, 0))
    return pl.pallas_call(
        _add_ln_kernel,
        grid=(n // tm,),
        in_specs=[row, row, vec, vec],
        out_specs=row,
        out_shape=jax.ShapeDtypeStruct((n, d), _F32),
        compiler_params=_params("arbitrary"),
        name="add_ln",
    )(h, mix, g.reshape(1, d), b.reshape(1, d))


ZX_DIM = D_INNER + CONV_DIM
PROJ_TM = LP // 4
PROJ_TN = ZX_DIM // 2
PROJ_CHUNK = 512
CONV_KEEP = 8


def _in_proj_kernel(x_ref, w_ref, cw_ref, cb_ref, o_ref, hist_ref):
    j = pl.program_id(0)
    seq_start = pl.program_id(1) % (LP // PROJ_TM) == 0
    x = x_ref[...].astype(_BF16)
    rows = lax.broadcasted_iota(jnp.int32, (PROJ_TM, PROJ_CHUNK), 0)
    real = jnp.logical_or(jnp.logical_not(seq_start), rows >= PAD)

    def plain(c):
        sl = slice(c * PROJ_CHUNK, (c + 1) * PROJ_CHUNK)
        o_ref[:, sl] = jnp.dot(x, w_ref[:, sl], preferred_element_type=_F32).astype(o_ref.dtype)

    def conv(c):
        sl = slice(c * PROJ_CHUNK, (c + 1) * PROJ_CHUNK)
        cur = jnp.where(real, jnp.dot(x, w_ref[:, sl], preferred_element_type=_F32), 0.0)
        prev = jnp.where(seq_start, 0.0, hist_ref[c])
        hist_ref[c] = cur[PROJ_TM - CONV_KEEP:, :]
        xx = jnp.concatenate([prev, cur], axis=0)
        w = cw_ref[:, sl]
        acc = cb_ref[:, sl] + w[CONV_WIDTH - 1:CONV_WIDTH, :] * cur
        for back in range(1, CONV_WIDTH):
            k = CONV_WIDTH - 1 - back
            acc = acc + w[k:k + 1, :] * pltpu.roll(xx, back, 0)[CONV_KEEP:, :]
        o_ref[:, sl] = (acc * _sigmoid(acc)).astype(o_ref.dtype)

    for block in range(ZX_DIM // PROJ_TN):
        @pl.when(j == block)
        def _(block=block):
            for c in range(PROJ_TN // PROJ_CHUNK):
                if block * PROJ_TN + c * PROJ_CHUNK < D_INNER:
                    plain(c)
                else:
                    conv(c)


def _in_proj(h, w_in, conv_w, conv_b):
    n, k = h.shape
    cw = jnp.pad(conv_w.astype(_F32), ((0, 0), (D_INNER, 0)))
    cb = jnp.pad(conv_b.astype(_F32), (D_INNER, 0)).reshape(1, ZX_DIM)
    return pl.pallas_call(
        _in_proj_kernel,
        grid=(ZX_DIM // PROJ_TN, n // PROJ_TM),
        in_specs=[pl.BlockSpec((PROJ_TM, k), lambda j, i: (i, 0)),
                  pl.BlockSpec((k, PROJ_TN), lambda j, i: (0, j)),
                  pl.BlockSpec((CONV_WIDTH, PROJ_TN), lambda j, i: (0, j)),
                  pl.BlockSpec((1, PROJ_TN), lambda j, i: (0, j))],
        out_specs=pl.BlockSpec((PROJ_TM, PROJ_TN), lambda j, i: (i, j)),
        out_shape=jax.ShapeDtypeStruct((n, ZX_DIM), _BF16),
        scratch_shapes=[pltpu.VMEM((PROJ_TN // PROJ_CHUNK, CONV_KEEP, PROJ_CHUNK), _F32)],
        compiler_params=_params("arbitrary", "arbitrary"),
        name="in_proj",
    )(h, w_in, cw, cb)


N_HEAD_PAIRS = SSM_HEADS // 2
PAIRS_PER_GROUP = N_HEAD_PAIRS // SSM_GROUPS


def _ssd_kernel(xs_ref, b_ref, c_ref, dtr_ref, dtb_ref, aneg_ref, dskip_ref, ltri_ref, exp_ref,
                y_ref, state_ref):
    c = pl.program_id(1)

    @pl.when(c == 0)
    def _():
        state_ref[...] = jnp.zeros_like(state_ref)

    rows = lax.broadcasted_iota(jnp.int32, (CHUNK, LANES), 0)
    cols = lax.broadcasted_iota(jnp.int32, (CHUNK, LANES), 1)
    causal = cols <= rows
    left = cols < SSM_HEAD_DIM

    dt = _softplus(dtr_ref[0] + dtb_ref[...])
    dt = jnp.where(rows + c * CHUNK >= PAD, dt, 0.0)
    a = dt * aneg_ref[...]
    a_cum = jnp.dot(ltri_ref[...], a, preferred_element_type=_F32,
                    precision=lax.Precision.HIGHEST)
    a_cum_t = a_cum.T
    total = a_cum[CHUNK - 1:CHUNK, :]
    from_start = jnp.exp(a_cum)
    to_end = jnp.exp(total - a_cum)
    chunk_decay = jnp.exp(jnp.broadcast_to(total, (8, LANES)))

    expand = exp_ref[...]
    dt_x = jnp.dot(dt.astype(_BF16), expand, preferred_element_type=_F32)
    to_end_x = jnp.dot(to_end.astype(_BF16), expand, preferred_element_type=_F32)
    chunk_decay_x = jnp.dot(chunk_decay.astype(_BF16), expand, preferred_element_type=_F32)

    for g in range(SSM_GROUPS):
        bg = b_ref[0, :, g * D_STATE:(g + 1) * D_STATE]
        cg = c_ref[0, :, g * D_STATE:(g + 1) * D_STATE]
        cb = lax.dot_general(cg, bg, (((1,), (1,)), ((), ())), preferred_element_type=_F32)
        bg_t = bg.astype(_F32).T.astype(_BF16)
        cg32 = cg.astype(_F32)
        for jj in range(PAIRS_PER_GROUP):
            j = g * PAIRS_PER_GROUP + jj
            sl = slice(j * LANES, (j + 1) * LANES)
            x2 = xs_ref[0, :, sl].astype(_F32)
            xdt = x2 * dt_x[:, sl]
            parts = []
            for h in (2 * j, 2 * j + 1):
                seg = a_cum[:, h:h + 1] - a_cum_t[h:h + 1, :]
                decay = jnp.where(causal, jnp.exp(seg), 0.0)
                parts.append((cb * decay).astype(_BF16))
            for h in (2 * j, 2 * j + 1):
                parts.append((cg32 * from_start[:, h:h + 1]).astype(_BF16))
            lhs = jnp.concatenate(parts, axis=1)
            state = state_ref[j]
            xb = xdt.astype(_BF16)
            sb = state.astype(_BF16)
            zero = jnp.zeros_like(xb)
            rhs = jnp.concatenate([jnp.where(left, xb, zero), jnp.where(left, zero, xb),
                                   jnp.where(left, sb, zero), jnp.where(left, zero, sb)], axis=0)
            y2 = jnp.dot(lhs, rhs, preferred_element_type=_F32)
            y2 = y2 + dskip_ref[:, sl] * x2
            y_ref[0, :, sl] = y2.astype(y_ref.dtype)
            upd = jnp.dot(bg_t, (xdt * to_end_x[:, sl]).astype(_BF16), preferred_element_type=_F32)
            state_ref[j] = state * chunk_decay_x[0:1, sl] + upd


def _ssd(proj, dt_raw, dt_bias, a_log, d_skip):
    pad_h = LANES - SSM_HEADS
    dtb = jnp.pad(dt_bias.astype(_F32), (0, pad_h)).reshape(1, LANES)
    aneg = jnp.pad(-jnp.exp(a_log.astype(_F32)), (0, pad_h)).reshape(1, LANES)
    dskip = jnp.repeat(d_skip.astype(_F32), SSM_HEAD_DIM).reshape(1, D_INNER)
    ltri = jnp.tril(jnp.ones((CHUNK, CHUNK), _F32))
    expand = (jnp.arange(LANES)[:, None] == (jnp.arange(D_INNER)[None, :] // SSM_HEAD_DIM)).astype(_BF16)
    gn = SSM_GROUPS * D_STATE
    vec = lambda n: pl.BlockSpec((1, n), lambda b, c: (0, 0))
    return pl.pallas_call(
        _ssd_kernel,
        grid=(BATCH, N_CHUNKS),
        in_specs=[pl.BlockSpec((1, CHUNK, D_INNER), lambda b, c: (b, c, 1)),
                  pl.BlockSpec((1, CHUNK, gn), lambda b, c: (b, c, 2 * D_INNER // gn)),
                  pl.BlockSpec((1, CHUNK, gn), lambda b, c: (b, c, 2 * D_INNER // gn + 1)),
                  pl.BlockSpec((1, CHUNK, LANES), lambda b, c: (b, c, 0)),
                  vec(LANES), vec(LANES), vec(D_INNER),
                  pl.BlockSpec((CHUNK, CHUNK), lambda b, c: (0, 0)),
                  pl.BlockSpec((LANES, D_INNER), lambda b, c: (0, 0))],
        out_specs=pl.BlockSpec((1, CHUNK, D_INNER), lambda b, c: (b, c, 0)),
        out_shape=jax.ShapeDtypeStruct((BATCH, LP, D_INNER), _BF16),
        scratch_shapes=[pltpu.VMEM((N_HEAD_PAIRS, D_STATE, LANES), _F32)],
        compiler_params=_params("arbitrary", "arbitrary"),
        name="ssd",
    )(proj, proj, proj, dt_raw, dtb, aneg, dskip, ltri, expand)


def _mamba_out_kernel(y_ref, z_ref, nw_ref, w_ref, h_ref, g_ref, b_ref, o_ref):
    y = y_ref[...].astype(_F32)
    z = z_ref[...].astype(_F32)
    yg = y * (z * _sigmoid(z))
    gw = D_INNER // SSM_GROUPS
    parts = []
    for g in range(SSM_GROUPS):
        blk = yg[:, g * gw:(g + 1) * gw]
        ms = jnp.mean(blk * blk, axis=-1, keepdims=True)
        parts.append(blk * lax.rsqrt(ms + LN_EPS))
    yn = jnp.concatenate(parts, axis=1) * nw_ref[...]
    mix = jnp.dot(yn.astype(_BF16), w_ref[...], preferred_element_type=_F32)
    o_ref[...] = _layer_norm(ALPHA * h_ref[...] + mix, g_ref[...], b_ref[...])


def _mamba_out(y, proj, norm_w, w_out, h, g, b, tm=256):
    n = h.shape[0]
    vec = lambda d: pl.BlockSpec((1, d), lambda i: (0, 0))
    return pl.pallas_call(
        _mamba_out_kernel,
        grid=(n // tm,),
        in_specs=[pl.BlockSpec((tm, D_INNER), lambda i: (i, 0)),
                  pl.BlockSpec((tm, D_INNER), lambda i: (i, 0)),
                  vec(D_INNER),
                  pl.BlockSpec((D_INNER, D_MODEL), lambda i: (0, 0)),
                  pl.BlockSpec((tm, D_MODEL), lambda i: (i, 0)),
                  vec(D_MODEL), vec(D_MODEL)],
        out_specs=pl.BlockSpec((tm, D_MODEL), lambda i: (i, 0)),
        out_shape=jax.ShapeDtypeStruct((n, D_MODEL), _F32),
        compiler_params=_params("arbitrary"),
        name="mamba_out",
    )(y, proj, norm_w.reshape(1, D_INNER).astype(_F32), w_out, h,
      g.reshape(1, D_MODEL), b.reshape(1, D_MODEL))


def _attn_out_kernel(o_ref_in, w_ref, h_ref, g_ref, b_ref, o_ref):
    mix = jnp.dot(o_ref_in[...], w_ref[...], preferred_element_type=_F32)
    o_ref[...] = _layer_norm(ALPHA * h_ref[...] + mix, g_ref[...], b_ref[...])


def _attn_out(o, w_o, h, g, b, tm=512):
    n = h.shape[0]
    vec = pl.BlockSpec((1, D_MODEL), lambda i: (0, 0))
    return pl.pallas_call(
        _attn_out_kernel,
        grid=(n // tm,),
        in_specs=[pl.BlockSpec((tm, SB_WIDTH), lambda i: (i, 0)),
                  pl.BlockSpec((SB_WIDTH, D_MODEL), lambda i: (0, 0)),
                  pl.BlockSpec((tm, D_MODEL), lambda i: (i, 0)),
                  vec, vec],
        out_specs=pl.BlockSpec((tm, D_MODEL), lambda i: (i, 0)),
        out_shape=jax.ShapeDtypeStruct((n, D_MODEL), _F32),
        compiler_params=_params("arbitrary"),
        name="attn_out",
    )(o, w_o, h, g.reshape(1, D_MODEL), b.reshape(1, D_MODEL))


SB_TQ = 128
SB_TK = 256
KV_FRONT = SB_TK - SB_TQ
LKV = LP + KV_FRONT


SB_HPS = 4
SB_LW = SB_HPS * SB_HEAD_DIM
N_QT = LP // SB_TQ
MASKED_SCORE = -1e30
SB_DEAD_LOG2 = 150.0
SB_BOUND_MARGIN = 8.0


def _sb_kernel(q_ref, k_ref, v_ref, tri_ref, o_ref,
               qm_ref, vm_ref, kbm_ref, lb_ref, sp_ref, w_ref, rs_ref, tot_ref):
    tri = tri_ref[...]
    col_minus_row = (lax.broadcasted_iota(jnp.int32, (SB_TQ, SB_TK), 1)
                     - lax.broadcasted_iota(jnp.int32, (SB_TQ, SB_TK), 0))

    def mask_q(c, carry):
        rows = pl.ds(pl.multiple_of(c * SB_TQ, SB_TQ), SB_TQ)
        blk = q_ref[0, rows, :]
        head = lax.broadcasted_iota(jnp.int32, blk.shape, 1) // SB_HEAD_DIM
        for h in range(SB_HPS):
            qm_ref[h, rows, :] = jnp.where(head == h, blk, jnp.zeros_like(blk))
        return carry

    def mask_v(c, carry):
        rows = pl.ds(pl.multiple_of(c * SB_TK, SB_TK), SB_TK)
        blk = v_ref[0, rows, :]
        head = lax.broadcasted_iota(jnp.int32, blk.shape, 1) // SB_HEAD_DIM
        for h in range(SB_HPS):
            vm_ref[h, rows, :] = jnp.where(head == h, blk, jnp.zeros_like(blk))
        kbar = jnp.mean(k_ref[0, rows, :].astype(_F32), axis=0, keepdims=True)
        mean_rows = jnp.broadcast_to(kbar, (LANES, SB_LW))
        row_id = lax.broadcasted_iota(jnp.int32, (LANES, SB_LW), 0)
        row_head = lax.broadcasted_iota(jnp.int32, (LANES, SB_LW), 1) // SB_HEAD_DIM
        kbm_ref[c] = jnp.where(row_id == row_head, mean_rows, 0.0).astype(_BF16)
        return carry

    lax.fori_loop(0, N_QT, mask_q, 0)
    lax.fori_loop(0, LKV // SB_TK, mask_v, 0)

    def stage_a(item, slot):
        qi, kb = item[0], item[1]
        q_rows = pl.ds(pl.multiple_of(qi * SB_TQ, SB_TQ), SB_TQ)
        kblk = k_ref[0, pl.ds(pl.multiple_of(kb * SB_TK, SB_TK), SB_TK), :]
        valid = col_minus_row < qi * SB_TQ + KV_FRONT - kb * SB_TK
        for h in range(SB_HPS):
            z = lax.dot_general(qm_ref[h, q_rows, :], kblk, (((1,), (1,)), ((), ())),
                                preferred_element_type=_F32)
            z = jnp.where(valid, z, MASKED_SCORE)
            neg_abs = pltpu.bitcast(pltpu.bitcast(z, jnp.uint32) | jnp.uint32(0x80000000), _F32)
            s = jnp.maximum(z, 0.0) + jnp.log2(1.0 + jnp.exp2(neg_abs))
            lb_ref[slot, h] = z - s
            sp_ref[slot, h] = s.astype(_BF16)
            rs_ref[slot, h] = jnp.broadcast_to(jnp.sum(s, axis=-1, keepdims=True), (SB_TQ, LANES))

    def stage_b(slot):
        for h in range(SB_HPS):
            later = jnp.dot(sp_ref[slot, h], tri, preferred_element_type=_F32)
            total = jnp.tile(tot_ref[h], (1, SB_TK // LANES))
            w_ref[slot, h] = jnp.exp2(lb_ref[slot, h] - later - total).astype(_BF16)

    def next_totals(item, prev_slot):
        fresh = item[2] == 1
        low = None
        for h in range(SB_HPS):
            tot = jnp.where(fresh, 0.0, tot_ref[h] + rs_ref[prev_slot, h])
            tot_ref[h] = tot
            low = tot if low is None else jnp.minimum(low, tot)
        return jnp.min(low) >= SB_DEAD_LOG2

    def next_block_dead(item):
        qi, kb, first, _ = item
        q = q_ref[0, pl.ds(pl.multiple_of(qi * SB_TQ, SB_TQ), SB_TQ), :]
        zbar = lax.dot_general(q, kbm_ref[kb], (((1,), (1,)), ((), ())),
                               preferred_element_type=_F32)
        zbar = zbar - 0.02 * jnp.abs(zbar) - 0.05
        bound = SB_TK * (jnp.maximum(zbar, 0.0) + jnp.log2(1.0 + jnp.exp2(-jnp.abs(zbar))))
        lane = lax.broadcasted_iota(jnp.int32, bound.shape, 1)
        low = None
        for h in range(SB_HPS):
            cand = tot_ref[h] + jnp.where(lane == h, bound, -MASKED_SCORE)
            low = cand if low is None else jnp.minimum(low, cand)
        return jnp.logical_and(first == 0, jnp.min(low) >= SB_DEAD_LOG2 + SB_BOUND_MARGIN)

    def stage_c(item, slot, acc):
        qi, kb, first, live = item
        k_rows = pl.ds(pl.multiple_of(kb * SB_TK, SB_TK), SB_TK)
        new = jnp.where(first == 1, 0.0, acc)
        for h in range(SB_HPS):
            new = new + jnp.dot(w_ref[slot, h], vm_ref[h, k_rows, :], preferred_element_type=_F32)
        acc = jnp.where(live == 1, new, acc)
        o_ref[0, pl.ds(pl.multiple_of(qi * SB_TQ, SB_TQ), SB_TQ), :] = acc.astype(o_ref.dtype)
        return acc

    def key_blocks(qi):
        return (qi * SB_TQ + SB_TQ + KV_FRONT + SB_TK - 1) // SB_TK

    def advance(item, dead):
        qi, kb, _, live = item
        tile_done = jnp.logical_or(kb == 0, dead)
        nxt = jnp.where(tile_done, qi + 1, qi)
        live = jnp.where(nxt >= N_QT, 0, live)
        nxt = jnp.minimum(nxt, N_QT - 1)
        kb = jnp.where(tile_done, key_blocks(nxt) - 1, kb - 1)
        return nxt, kb, tile_done.astype(jnp.int32), live

    def step(items, slot, acc):
        cur, prev, prev2 = items
        acc = stage_c(prev2, slot, acc)
        stage_b(1 - slot)
        dead = jnp.logical_or(next_totals(cur, 1 - slot), next_block_dead(cur))
        stage_a(cur, slot)
        return (advance(cur, dead), cur, prev), acc

    def four_steps(carry):
        items, acc = carry
        for slot in (0, 1, 0, 1):
            items, acc = step(items, slot, acc)
        return items, acc

    def unfinished(carry):
        (cur, prev, prev2), _ = carry
        return cur[3] + prev[3] + prev2[3] > 0

    as_item = lambda *v: tuple(jnp.int32(x) for x in v)
    item0 = as_item(0, key_blocks(0) - 1, 1, 1)
    item1 = as_item(1, key_blocks(1) - 1, 1, 1)
    tot_ref[...] = jnp.zeros_like(tot_ref)
    stage_a(item0, 0)
    stage_b(0)
    dead = next_totals(item1, 0)
    stage_a(item1, 1)
    carry = ((advance(item1, dead), item1, item0), jnp.zeros((SB_TQ, SB_LW), _F32))
    lax.while_loop(unfinished, four_steps, carry)


def _stick_breaking(q, kv):
    j = jnp.arange(SB_TK)
    tri = (j[:, None] > j[None, :]).astype(_BF16)
    n_steps = SB_WIDTH // SB_LW
    slots = lambda dtype: pltpu.VMEM((2, SB_HPS, SB_TQ, SB_TK), dtype)
    return pl.pallas_call(
        _sb_kernel,
        grid=(BATCH, n_steps),
        in_specs=[pl.BlockSpec((1, LP, SB_LW), lambda b, p: (b, 0, p)),
                  pl.BlockSpec((1, LKV, SB_LW), lambda b, p: (b, 0, p)),
                  pl.BlockSpec((1, LKV, SB_LW), lambda b, p: (b, 0, p + n_steps)),
                  pl.BlockSpec((SB_TK, SB_TK), lambda b, p: (0, 0))],
        out_specs=pl.BlockSpec((1, LP, SB_LW), lambda b, p: (b, 0, p)),
        scratch_shapes=[pltpu.VMEM((SB_HPS, LP, SB_LW), _BF16),
                        pltpu.VMEM((SB_HPS, LKV, SB_LW), _BF16),
                        pltpu.VMEM((LKV // SB_TK, LANES, SB_LW), _BF16),
                        slots(_F32), slots(_BF16), slots(_BF16),
                        pltpu.VMEM((2, SB_HPS, SB_TQ, LANES), _F32),
                        pltpu.VMEM((SB_HPS, SB_TQ, LANES), _F32)],
        out_shape=jax.ShapeDtypeStruct((BATCH, LP, SB_WIDTH), _BF16),
        compiler_params=_params("arbitrary", "arbitrary"),
        name="stick_breaking",
    )(q, kv, kv, tri)


ROUTER_TM = 512


def _router_kernel(h_ref, w_ref, b_ref, tri_ref, cls_ref, rank_ref, cnt_ref, gext_ref, base_ref):
    @pl.when(pl.program_id(0) == 0)
    def _():
        base_ref[...] = jnp.zeros_like(base_ref)

    logits = lax.dot_general(w_ref[...], h_ref[...], (((1,), (1,)), ((), ())),
                             preferred_element_type=_F32,
                             precision=lax.Precision.HIGHEST) + b_ref[...]
    m = jnp.max(logits, axis=0, keepdims=True)
    e = jnp.exp(logits - m)
    p = e / jnp.sum(e, axis=0, keepdims=True)
    ng = N_EXPERT_GROUPS
    pj = [p[j * ng:(j + 1) * ng, :] for j in range(EXPERTS_PER_GROUP)]
    m1 = jnp.maximum(jnp.maximum(pj[0], pj[1]), jnp.maximum(pj[2], pj[3]))
    i1 = jnp.where(pj[0] == m1, 0, jnp.where(pj[1] == m1, 1, jnp.where(pj[2] == m1, 2, 3)))
    qj = [jnp.where(i1 == j, -1.0, pj[j]) for j in range(EXPERTS_PER_GROUP)]
    m2 = jnp.maximum(jnp.maximum(qj[0], qj[1]), jnp.maximum(qj[2], qj[3]))
    i2 = jnp.where(qj[0] == m2, 0, jnp.where(qj[1] == m2, 1, jnp.where(qj[2] == m2, 2, 3)))
    score = m1 + m2
    gid = lax.broadcasted_iota(jnp.int32, score.shape, 0)
    best = jnp.max(score, axis=0, keepdims=True)
    gsel = jnp.min(jnp.where(score == best, gid, ng), axis=0, keepdims=True)
    chosen = gid == gsel
    pick_f = lambda v: jnp.sum(jnp.where(chosen, v, 0.0), axis=0, keepdims=True)
    pick_i = lambda v: jnp.sum(jnp.where(chosen, v, 0), axis=0, keepdims=True)
    v1, v2, j1, j2 = pick_f(m1), pick_f(m2), pick_i(i1), pick_i(i2)
    denom = v1 + v2
    g1 = v1 / denom
    g2 = v2 / denom
    first_low = j1 < j2
    lo = jnp.where(first_low, j1, j2)
    hi = jnp.where(first_low, j2, j1)
    base = jnp.where(lo == 0, 0, jnp.where(lo == 1, 3, 5))
    cls = gsel * N_PAIR_CLASSES + base + hi - lo - 1
    cls_ref[...] = cls

    member = lax.broadcasted_iota(jnp.int32, (CLS_ROWS, ROUTER_TM), 0) == cls
    onehot = jnp.where(member, 1.0, 0.0)
    upto = jnp.dot(onehot.astype(_BF16), tri_ref[...], preferred_element_type=_F32)
    before = base_ref[...]
    rank = jnp.sum(jnp.where(member, upto - 1.0 + before, 0.0), axis=0, keepdims=True)
    rank_ref[...] = rank.astype(jnp.int32)
    after = before + jnp.sum(onehot, axis=1, keepdims=True)
    base_ref[...] = after
    cnt_ref[...] = after[:, :LANES]

    gates = jnp.concatenate([jnp.where(first_low, g1, g2), jnp.where(first_low, g2, g1),
                             jnp.zeros((LANES - 2, ROUTER_TM), _F32)], axis=0)
    gext_ref[...] = gates.T


def _router(h, router_w, router_b):
    n = h.shape[0]
    perm = jnp.array([g * EXPERTS_PER_GROUP + j for j in range(EXPERTS_PER_GROUP)
                      for g in range(N_EXPERT_GROUPS)])
    w_t = router_w.astype(_F32).T[perm]
    b_t = router_b.astype(_F32)[perm].reshape(N_EXPERTS, 1)
    j = jnp.arange(ROUTER_TM)
    tri = (j[:, None] <= j[None, :]).astype(_BF16)
    out = pl.BlockSpec((1, ROUTER_TM), lambda i: (0, i))
    return pl.pallas_call(
        _router_kernel,
        grid=(n // ROUTER_TM,),
        in_specs=[pl.BlockSpec((ROUTER_TM, D_MODEL), lambda i: (i, 0)),
                  pl.BlockSpec((N_EXPERTS, D_MODEL), lambda i: (0, 0)),
                  pl.BlockSpec((N_EXPERTS, 1), lambda i: (0, 0)),
                  pl.BlockSpec((ROUTER_TM, ROUTER_TM), lambda i: (0, 0))],
        out_specs=[out, out,
                   pl.BlockSpec((CLS_ROWS, LANES), lambda i: (0, 0)),
                   pl.BlockSpec((ROUTER_TM, LANES), lambda i: (i, 0))],
        out_shape=[jax.ShapeDtypeStruct((1, n), jnp.int32),
                   jax.ShapeDtypeStruct((1, n), jnp.int32),
                   jax.ShapeDtypeStruct((CLS_ROWS, LANES), _F32),
                   jax.ShapeDtypeStruct((n, LANES), _F32)],
        scratch_shapes=[pltpu.VMEM((CLS_ROWS, ROUTER_TM), _F32)],
        compiler_params=_params("arbitrary"),
        name="router",
    )(h, w_t, b_t, tri)


def _class_experts():
    lo, hi = [], []
    for g in range(N_EXPERT_GROUPS):
        for a in range(EXPERTS_PER_GROUP):
            for b in range(a + 1, EXPERTS_PER_GROUP):
                lo.append(g * EXPERTS_PER_GROUP + a)
                hi.append(g * EXPERTS_PER_GROUP + b)
    return jnp.array(lo, jnp.int32), jnp.array(hi, jnp.int32)


def _expert_kernel(elo_ref, ehi_ref, nused_ref, x_ref, wg1, wu1, wd1, wg2, wu2, wd2, o_ref):
    t = pl.program_id(0)

    @pl.when(t < nused_ref[0])
    def _():
        x = x_ref[:, :D_MODEL].astype(_BF16)
        gates = x_ref[:, D_MODEL:]

        def ffn(wg, wu, wd, gate):
            a = jnp.dot(x, wg[0], preferred_element_type=_F32)
            u = jnp.dot(x, wu[0], preferred_element_type=_F32)
            hid = (a * _sigmoid(a)) * u * gate
            return jnp.dot(hid.astype(_BF16), wd[0], preferred_element_type=_F32)

        o_ref[...] = ffn(wg1, wu1, wd1, gates[:, 0:1]) + ffn(wg2, wu2, wd2, gates[:, 1:2])

    @pl.when(t >= nused_ref[0])
    def _():
        o_ref[...] = jnp.zeros_like(o_ref)


def _experts(x_sorted, tile_lo, tile_hi, n_used, w_gate, w_up, w_down):
    up_spec = lambda which: pl.BlockSpec(
        (1, D_MODEL, D_EXPERT), lambda t, lo, hi, nu: ((lo, hi)[which][t], 0, 0))
    down_spec = lambda which: pl.BlockSpec(
        (1, D_EXPERT, D_MODEL), lambda t, lo, hi, nu: ((lo, hi)[which][t], 0, 0))
    row = lambda d: pl.BlockSpec((MOE_TM, d), lambda t, lo, hi, nu: (t, 0))
    grid_spec = pltpu.PrefetchScalarGridSpec(
        num_scalar_prefetch=3,
        grid=(MOE_TILES,),
        in_specs=[row(XROW), up_spec(0), up_spec(0), down_spec(0),
                  up_spec(1), up_spec(1), down_spec(1)],
        out_specs=row(D_MODEL),
    )
    return pl.pallas_call(
        _expert_kernel,
        grid_spec=grid_spec,
        out_shape=jax.ShapeDtypeStruct((MOE_ROWS, D_MODEL), _F32),
        compiler_params=_params("arbitrary"),
        name="experts",
    )(tile_lo, tile_hi, n_used, x_sorted, w_gate, w_up, w_down, w_gate, w_up, w_down)


def _issue_rows(row_copy, pos_ref, base, slot):
    def issue(g, carry):
        for u in range(ROW_UNROLL):
            r = g * ROW_UNROLL + u
            row_copy(slot, r, pos_ref[base + r]).start(priority=u % 2)
        return carry
    lax.fori_loop(0, ROW_TM // ROW_UNROLL, issue, 0)


def _dispatch_kernel(pos_ref, h_ref, g_ref, init_ref, out_ref, stage_ref, sem):
    del init_ref
    s = pl.program_id(0)
    last = pl.num_programs(0) - 1
    slot = s % 2

    def row_copy(slot, r, p):
        return pltpu.make_async_copy(stage_ref.at[slot, pl.ds(r, 1), :],
                                     out_ref.at[pl.ds(p, 1), :], sem.at[slot])

    def wait_slot(slot):
        pltpu.make_async_copy(stage_ref.at[slot], out_ref.at[pl.ds(0, ROW_TM), :],
                              sem.at[slot]).wait()

    @pl.when(s >= 2)
    def _():
        wait_slot(slot)

    stage_ref[slot, :, :D_MODEL] = h_ref[...]
    stage_ref[slot, :, D_MODEL:] = g_ref[...]
    _issue_rows(row_copy, pos_ref, s * ROW_TM, slot)

    @pl.when(s == last)
    def _():
        wait_slot(slot)
        wait_slot(1 - slot)


def _dispatch(pos, h, gext):
    n = h.shape[0]
    assert n // ROW_TM >= 2
    grid_spec = pltpu.PrefetchScalarGridSpec(
        num_scalar_prefetch=1,
        grid=(n // ROW_TM,),
        in_specs=[pl.BlockSpec((ROW_TM, D_MODEL), lambda i, pos: (i, 0)),
                  pl.BlockSpec((ROW_TM, LANES), lambda i, pos: (i, 0)),
                  pl.BlockSpec(memory_space=pl.ANY)],
        out_specs=pl.BlockSpec(memory_space=pl.ANY),
        scratch_shapes=[pltpu.VMEM((2, ROW_TM, XROW), _F32), pltpu.SemaphoreType.DMA((2,))],
    )
    return pl.pallas_call(
        _dispatch_kernel,
        grid_spec=grid_spec,
        out_shape=jax.ShapeDtypeStruct((MOE_ROWS, XROW), _F32),
        input_output_aliases={3: 0},
        compiler_params=_params("arbitrary"),
        name="dispatch",
    )(pos, h, gext, jnp.zeros((MOE_ROWS, XROW), _F32))


def _combine_kernel(pos_ref, h_ref, y_ref, g_ref, b_ref, o_ref, buf_ref, sem):
    s = pl.program_id(0)
    last = pl.num_programs(0) - 1
    slot = s % 2

    def row_copy(slot, r, p):
        return pltpu.make_async_copy(y_ref.at[pl.ds(p, 1), :],
                                     buf_ref.at[slot, pl.ds(r, 1), :], sem.at[slot])

    @pl.when(s == 0)
    def _():
        _issue_rows(row_copy, pos_ref, 0, 0)

    @pl.when(s < last)
    def _():
        _issue_rows(row_copy, pos_ref, (s + 1) * ROW_TM, 1 - slot)

    pltpu.make_async_copy(y_ref.at[pl.ds(0, ROW_TM), :], buf_ref.at[slot], sem.at[slot]).wait()
    o_ref[...] = _layer_norm(ALPHA * h_ref[...] + buf_ref[slot], g_ref[...], b_ref[...])


def _combine(pos, h, y_sorted, g, b):
    n = h.shape[0]
    row = pl.BlockSpec((ROW_TM, D_MODEL), lambda i, pos: (i, 0))
    vec = pl.BlockSpec((1, D_MODEL), lambda i, pos: (0, 0))
    grid_spec = pltpu.PrefetchScalarGridSpec(
        num_scalar_prefetch=1,
        grid=(n // ROW_TM,),
        in_specs=[row, pl.BlockSpec(memory_space=pl.ANY), vec, vec],
        out_specs=row,
        scratch_shapes=[pltpu.VMEM((2, ROW_TM, D_MODEL), _F32), pltpu.SemaphoreType.DMA((2,))],
    )
    return pl.pallas_call(
        _combine_kernel,
        grid_spec=grid_spec,
        out_shape=jax.ShapeDtypeStruct((n, D_MODEL), _F32),
        compiler_params=_params("arbitrary"),
        name="combine",
    )(pos, h, y_sorted, g.reshape(1, D_MODEL), b.reshape(1, D_MODEL))


def _moe_layer(h, router_w, router_b, w_gate, w_up, w_down, ln_g, ln_b):
    cls, rank, cnt, gext = _router(h, router_w, router_b)
    counts = cnt[:N_CLASSES, 0].astype(jnp.int32)
    tiles = (counts + MOE_TM - 1) // MOE_TM
    tile_end = jnp.cumsum(tiles)
    class_start = (tile_end - tiles) * MOE_TM
    classes = jnp.arange(N_CLASSES, dtype=jnp.int32)
    pos = jnp.sum(jnp.where(cls[0][:, None] == classes[None, :], class_start[None, :], 0),
                  axis=1) + rank[0]
    n_used = tile_end[-1]
    tile_ids = jnp.minimum(jnp.arange(MOE_TILES, dtype=jnp.int32), n_used - 1)
    tile_cls = jnp.sum((tile_ids[:, None] >= tile_end[None, :]).astype(jnp.int32), axis=1)
    class_lo, class_hi = _class_experts()
    x_sorted = _dispatch(pos, h, gext)
    y_sorted = _experts(x_sorted, class_lo[tile_cls], class_hi[tile_cls],
                        n_used.reshape(1).astype(jnp.int32), w_gate, w_up, w_down)
    return _combine(pos, h, y_sorted, ln_g, ln_b)


def kernel(x, meta_tokens, mamba_w_in, mamba_conv_w, mamba_conv_b, mamba_dt_bias, mamba_a_log,
           mamba_d_skip, mamba_norm_w, mamba_w_out, sb_w_q, sb_w_o, shared_w_k, shared_w_v,
           ln_mix_g, ln_mix_b, ln_ffn_g, ln_ffn_b, router_w, router_b, moe_w_gate, moe_w_up,
           moe_w_down):
    bsz = x.shape[0]
    meta = jnp.broadcast_to(meta_tokens.astype(x.dtype)[None], (bsz, N_META, D_MODEL))
    h = jnp.concatenate([jnp.zeros((bsz, PAD, D_MODEL), x.dtype), meta, x], axis=1)
    h = h.reshape(NP, D_MODEL)
    kv = None
    for layer in range(DEPTH):
        if layer < N_A_LAYERS:
            w_in = mamba_w_in[layer].astype(_BF16)
            w_dt = jnp.pad(w_in[:, ZX_DIM:], ((0, 0), (0, LANES - SSM_HEADS)))
            proj = _in_proj(h, w_in, mamba_conv_w[layer], mamba_conv_b[layer])
            dt_raw = _matmul(h, w_dt, LANES, 512, LANES, _F32)
            y = _ssd(proj.reshape(BATCH, LP, ZX_DIM), dt_raw.reshape(BATCH, LP, LANES),
                     mamba_dt_bias[layer], mamba_a_log[layer], mamba_d_skip[layer])
            h = _mamba_out(y.reshape(NP, D_INNER), proj, mamba_norm_w[layer],
                           mamba_w_out[layer].astype(_BF16), h, ln_mix_g[layer], ln_mix_b[layer])
        else:
            j = layer - N_A_LAYERS
            if kv is None:
                w_kv = jnp.concatenate([shared_w_k, shared_w_v], axis=1).astype(_BF16)
                kv = _matmul(h, w_kv, 2 * SB_WIDTH, 512, 2 * SB_WIDTH, _BF16).reshape(BATCH, LP, 2 * SB_WIDTH)
                kv = jnp.pad(kv[:, PAD:], ((0, 0), (PAD + KV_FRONT, 0), (0, 0)))
            q = _matmul(h, sb_w_q[j].astype(_BF16), SB_WIDTH, 512, 1024, _BF16,
                        scale=SB_SCALE * LOG2_E)
            o = _stick_breaking(q.reshape(BATCH, LP, SB_WIDTH), kv)
            h = _attn_out(o.reshape(NP, SB_WIDTH), sb_w_o[j].astype(_BF16), h,
                          ln_mix_g[layer], ln_mix_b[layer])
        h = _moe_layer(h, router_w, router_b, moe_w_gate[layer].astype(_BF16),
                       moe_w_up[layer].astype(_BF16), moe_w_down[layer].astype(_BF16),
                       ln_ffn_g[layer], ln_ffn_b[layer])
    return h.reshape(bsz, LP, D_MODEL)[:, PAD + N_META:]
```

```python
import functools
import math

import jax
import jax.numpy as jnp
from jax import lax
from jax.experimental import pallas as pl
from jax.experimental.pallas import tpu as pltpu

D_MODEL = 1024
BATCH = 8
SEQ = 2048
DEPTH = 4
N_META = 16
N_A_LAYERS = DEPTH // 2
ALPHA = (2.0 * DEPTH) ** 0.25
LN_EPS = 1e-5
D_INNER = 2048
SSM_HEAD_DIM = 64
SSM_HEADS = 32
SSM_GROUPS = 4
D_STATE = 128
CONV_WIDTH = 4
CHUNK = 128
CONV_DIM = D_INNER + 2 * SSM_GROUPS * D_STATE
SB_HEADS = 16
SB_HEAD_DIM = 64
SB_WIDTH = SB_HEADS * SB_HEAD_DIM
SB_SCALE = SB_HEAD_DIM ** -0.5
LOG2_E = math.log2(math.e)
N_EXPERTS = 16
N_EXPERT_GROUPS = 4
EXPERTS_PER_GROUP = 4
D_EXPERT = 512

L_REAL = N_META + SEQ
PAD = (-L_REAL) % CHUNK
LP = L_REAL + PAD
N_CHUNKS = LP // CHUNK
NP = BATCH * LP
LANES = 128
N_PAIR_CLASSES = 6
N_CLASSES = N_EXPERT_GROUPS * N_PAIR_CLASSES
MOE_TM = 256
MOE_TILES = NP // MOE_TM + N_CLASSES
MOE_ROWS = MOE_TILES * MOE_TM
CLS_ROWS = 32
XROW = D_MODEL + LANES
ROW_TM = 512
VMEM_LIMIT = 48 * 1024 * 1024

_F32 = jnp.float32
_BF16 = jnp.bfloat16


def _params(*sem):
    return pltpu.CompilerParams(dimension_semantics=sem, vmem_limit_bytes=VMEM_LIMIT)


def _sigmoid(x):
    return 1.0 / (1.0 + jnp.exp(-x))


def _softplus(x):
    return jnp.maximum(x, 0.0) + jnp.log(1.0 + jnp.exp(-jnp.abs(x)))


def _layer_norm(t, g, b):
    mu = jnp.mean(t, axis=-1, keepdims=True)
    d = t - mu
    var = jnp.mean(d * d, axis=-1, keepdims=True)
    return d * lax.rsqrt(var + LN_EPS) * g + b


def _mm_kernel(x_ref, w_ref, o_ref, *, scale):
    x = x_ref[...].astype(_BF16)
    acc = jnp.dot(x, w_ref[...], preferred_element_type=_F32)
    if scale != 1.0:
        acc = acc * scale
    o_ref[...] = acc.astype(o_ref.dtype)


def _matmul(x, w, n_out, tm, tn, out_dtype, scale=1.0):
    m, k = x.shape
    return pl.pallas_call(
        functools.partial(_mm_kernel, scale=scale),
        grid=(n_out // tn, m // tm),
        in_specs=[pl.BlockSpec((tm, k), lambda j, i: (i, 0)),
                  pl.BlockSpec((k, tn), lambda j, i: (0, j))],
        out_specs=pl.BlockSpec((tm, tn), lambda j, i: (i, j)),
        out_shape=jax.ShapeDtypeStruct((m, n_out), out_dtype),
        compiler_params=_params("arbitrary", "arbitrary"),
        name="matmul",
    )(x, w)


def _add_ln_kernel(h_ref, m_ref, g_ref, b_ref, o_ref):
    t = ALPHA * h_ref[...] + m_ref[...]
    o_ref[...] = _layer_norm(t, g_ref[...], b_ref[...])


def _add_ln(h, mix, g, b, tm=512):
    n, d = h.shape
    row = pl.BlockSpec((tm, d), lambda i: (i, 0))
    vec = pl.BlockSpec((1, d), lambda i: (0, 0))
    return pl.pallas_call(
        _add_ln_kernel,
        grid=(n // tm,),
        in_specs=[row, row, vec, vec],
        out_specs=row,
        out_shape=jax.ShapeDtypeStruct((n, d), _F32),
        compiler_params=_params("arbitrary"),
        name="add_ln",
    )(h, mix, g.reshape(1, d), b.reshape(1, d))


ZX_DIM = D_INNER + CONV_DIM
PROJ_TM = LP // 4
PROJ_TN = ZX_DIM // 2
PROJ_CHUNK = 512
CONV_KEEP = 8


def _in_proj_kernel(x_ref, w_ref, cw_ref, cb_ref, o_ref, hist_ref):
    j = pl.program_id(0)
    seq_start = pl.program_id(1) % (LP // PROJ_TM) == 0
    x = x_ref[...].astype(_BF16)
    rows = lax.broadcasted_iota(jnp.int32, (PROJ_TM, PROJ_CHUNK), 0)
    real = jnp.logical_or(jnp.logical_not(seq_start), rows >= PAD)

    def plain(c):
        sl = slice(c * PROJ_CHUNK, (c + 1) * PROJ_CHUNK)
        o_ref[:, sl] = jnp.dot(x, w_ref[:, sl], preferred_element_type=_F32).astype(o_ref.dtype)

    def conv(c):
        sl = slice(c * PROJ_CHUNK, (c + 1) * PROJ_CHUNK)
        cur = jnp.where(real, jnp.dot(x, w_ref[:, sl], preferred_element_type=_F32), 0.0)
        prev = jnp.where(seq_start, 0.0, hist_ref[c])
        hist_ref[c] = cur[PROJ_TM - CONV_KEEP:, :]
        xx = jnp.concatenate([prev, cur], axis=0)
        w = cw_ref[:, sl]
        acc = cb_ref[:, sl] + w[CONV_WIDTH - 1:CONV_WIDTH, :] * cur
        for back in range(1, CONV_WIDTH):
            k = CONV_WIDTH - 1 - back
            acc = acc + w[k:k + 1, :] * pltpu.roll(xx, back, 0)[CONV_KEEP:, :]
        o_ref[:, sl] = (acc * _sigmoid(acc)).astype(o_ref.dtype)

    for block in range(ZX_DIM // PROJ_TN):
        @pl.when(j == block)
        def _(block=block):
            for c in range(PROJ_TN // PROJ_CHUNK):
                if block * PROJ_TN + c * PROJ_CHUNK < D_INNER:
                    plain(c)
                else:
                    conv(c)


def _in_proj(h, w_in, conv_w, conv_b):
    n, k = h.shape
    cw = jnp.pad(conv_w.astype(_F32), ((0, 0), (D_INNER, 0)))
    cb = jnp.pad(conv_b.astype(_F32), (D_INNER, 0)).reshape(1, ZX_DIM)
    return pl.pallas_call(
        _in_proj_kernel,
        grid=(ZX_DIM // PROJ_TN, n // PROJ_TM),
        in_specs=[pl.BlockSpec((PROJ_TM, k), lambda j, i: (i, 0)),
                  pl.BlockSpec((k, PROJ_TN), lambda j, i: (0, j)),
                  pl.BlockSpec((CONV_WIDTH, PROJ_TN), lambda j, i: (0, j)),
                  pl.BlockSpec((1, PROJ_TN), lambda j, i: (0, j))],
        out_specs=pl.BlockSpec((PROJ_TM, PROJ_TN), lambda j, i: (i, j)),
        out_shape=jax.ShapeDtypeStruct((n, ZX_DIM), _BF16),
        scratch_shapes=[pltpu.VMEM((PROJ_TN // PROJ_CHUNK, CONV_KEEP, PROJ_CHUNK), _F32)],
        compiler_params=_params("arbitrary", "arbitrary"),
        name="in_proj",
    )(h, w_in, cw, cb)


N_HEAD_PAIRS = SSM_HEADS // 2
PAIRS_PER_GROUP = N_HEAD_PAIRS // SSM_GROUPS


def _ssd_kernel(xs_ref, b_ref, c_ref, dtr_ref, dtb_ref, aneg_ref, dskip_ref, ltri_ref, exp_ref,
                y_ref, state_ref):
    c = pl.program_id(1)

    @pl.when(c == 0)
    def _():
        state_ref[...] = jnp.zeros_like(state_ref)

    rows = lax.broadcasted_iota(jnp.int32, (CHUNK, LANES), 0)
    cols = lax.broadcasted_iota(jnp.int32, (CHUNK, LANES), 1)
    causal = cols <= rows
    left = cols < SSM_HEAD_DIM

    dt = _softplus(dtr_ref[0] + dtb_ref[...])
    dt = jnp.where(rows + c * CHUNK >= PAD, dt, 0.0)
    a = dt * aneg_ref[...]
    a_cum = jnp.dot(ltri_ref[...], a, preferred_element_type=_F32,
                    precision=lax.Precision.HIGHEST)
    a_cum_t = a_cum.T
    total = a_cum[CHUNK - 1:CHUNK, :]
    to_end = jnp.exp(total - a_cum)
    chunk_decay = jnp.exp(jnp.broadcast_to(total, (8, LANES)))

    expand = exp_ref[...]
    dt_x = jnp.dot(dt.astype(_BF16), expand, preferred_element_type=_F32)
    to_end_x = jnp.dot(to_end.astype(_BF16), expand, preferred_element_type=_F32)
    chunk_decay_x = jnp.dot(chunk_decay.astype(_BF16), expand, preferred_element_type=_F32)

    for g in range(SSM_GROUPS):
        bg = b_ref[0, :, g * D_STATE:(g + 1) * D_STATE]
        cg = c_ref[0, :, g * D_STATE:(g + 1) * D_STATE]
        cb = lax.dot_general(cg, bg, (((1,), (1,)), ((), ())), preferred_element_type=_F32)
        bg_t = bg.astype(_F32).T.astype(_BF16)
        cg32 = cg.astype(_F32)
        for jj in range(PAIRS_PER_GROUP):
            j = g * PAIRS_PER_GROUP + jj
            sl = slice(j * LANES, (j + 1) * LANES)
            x2 = xs_ref[0, :, sl].astype(_F32)
            xdt = x2 * dt_x[:, sl]
            parts, from_start = [], []
            for h in (2 * j, 2 * j + 1):
                a_col = jnp.broadcast_to(a_cum[:, h:h + 1], (CHUNK, CHUNK))
                decay = jnp.where(causal, jnp.exp(a_col - a_cum_t[h:h + 1, :]), 0.0)
                parts.append((cb * decay).astype(_BF16))
                from_start.append(jnp.exp(a_col))
            for fs in from_start:
                parts.append((cg32 * fs).astype(_BF16))
            lhs = jnp.concatenate(parts, axis=1)
            state = state_ref[j]
            xb = xdt.astype(_BF16)
            sb = state.astype(_BF16)
            zero = jnp.zeros_like(xb)
            rhs = jnp.concatenate([jnp.where(left, xb, zero), jnp.where(left, zero, xb),
                                   jnp.where(left, sb, zero), jnp.where(left, zero, sb)], axis=0)
            y2 = jnp.dot(lhs, rhs, preferred_element_type=_F32)
            y2 = y2 + dskip_ref[:, sl] * x2
            y_ref[0, :, sl] = y2.astype(y_ref.dtype)
            upd = jnp.dot(bg_t, (xdt * to_end_x[:, sl]).astype(_BF16), preferred_element_type=_F32)
            state_ref[j] = state * chunk_decay_x[0:1, sl] + upd


def _ssd(proj, dt_raw, dt_bias, a_log, d_skip):
    pad_h = LANES - SSM_HEADS
    dtb = jnp.pad(dt_bias.astype(_F32), (0, pad_h)).reshape(1, LANES)
    aneg = jnp.pad(-jnp.exp(a_log.astype(_F32)), (0, pad_h)).reshape(1, LANES)
    dskip = jnp.repeat(d_skip.astype(_F32), SSM_HEAD_DIM).reshape(1, D_INNER)
    ltri = jnp.tril(jnp.ones((CHUNK, CHUNK), _F32))
    expand = (jnp.arange(LANES)[:, None] == (jnp.arange(D_INNER)[None, :] // SSM_HEAD_DIM)).astype(_BF16)
    gn = SSM_GROUPS * D_STATE
    vec = lambda n: pl.BlockSpec((1, n), lambda b, c: (0, 0))
    return pl.pallas_call(
        _ssd_kernel,
        grid=(BATCH, N_CHUNKS),
        in_specs=[pl.BlockSpec((1, CHUNK, D_INNER), lambda b, c: (b, c, 1)),
                  pl.BlockSpec((1, CHUNK, gn), lambda b, c: (b, c, 2 * D_INNER // gn)),
                  pl.BlockSpec((1, CHUNK, gn), lambda b, c: (b, c, 2 * D_INNER // gn + 1)),
                  pl.BlockSpec((1, CHUNK, LANES), lambda b, c: (b, c, 0)),
                  vec(LANES), vec(LANES), vec(D_INNER),
                  pl.BlockSpec((CHUNK, CHUNK), lambda b, c: (0, 0)),
                  pl.BlockSpec((LANES, D_INNER), lambda b, c: (0, 0))],
        out_specs=pl.BlockSpec((1, CHUNK, D_INNER), lambda b, c: (b, c, 0)),
        out_shape=jax.ShapeDtypeStruct((BATCH, LP, D_INNER), _BF16),
        scratch_shapes=[pltpu.VMEM((N_HEAD_PAIRS, D_STATE, LANES), _F32)],
        compiler_params=_params("arbitrary", "arbitrary"),
        name="ssd",
    )(proj, proj, proj, dt_raw, dtb, aneg, dskip, ltri, expand)


def _mamba_out_kernel(y_ref, z_ref, nw_ref, w_ref, h_ref, g_ref, b_ref, o_ref):
    y = y_ref[...].astype(_F32)
    z = z_ref[...].astype(_F32)
    yg = y * (z * _sigmoid(z))
    gw = D_INNER // SSM_GROUPS
    parts = []
    for g in range(SSM_GROUPS):
        blk = yg[:, g * gw:(g + 1) * gw]
        ms = jnp.mean(blk * blk, axis=-1, keepdims=True)
        parts.append(blk * lax.rsqrt(ms + LN_EPS))
    yn = jnp.concatenate(parts, axis=1) * nw_ref[...]
    mix = jnp.dot(yn.astype(_BF16), w_ref[...], preferred_element_type=_F32)
    o_ref[...] = _layer_norm(ALPHA * h_ref[...] + mix, g_ref[...], b_ref[...])


def _mamba_out(y, proj, norm_w, w_out, h, g, b, tm=256):
    n = h.shape[0]
    vec = lambda d: pl.BlockSpec((1, d), lambda i: (0, 0))
    return pl.pallas_call(
        _mamba_out_kernel,
        grid=(n // tm,),
        in_specs=[pl.BlockSpec((tm, D_INNER), lambda i: (i, 0)),
                  pl.BlockSpec((tm, D_INNER), lambda i: (i, 0)),
                  vec(D_INNER),
                  pl.BlockSpec((D_INNER, D_MODEL), lambda i: (0, 0)),
                  pl.BlockSpec((tm, D_MODEL), lambda i: (i, 0)),
                  vec(D_MODEL), vec(D_MODEL)],
        out_specs=pl.BlockSpec((tm, D_MODEL), lambda i: (i, 0)),
        out_shape=jax.ShapeDtypeStruct((n, D_MODEL), _F32),
        compiler_params=_params("arbitrary"),
        name="mamba_out",
    )(y, proj, norm_w.reshape(1, D_INNER).astype(_F32), w_out, h,
      g.reshape(1, D_MODEL), b.reshape(1, D_MODEL))


def _attn_out_kernel(o_ref_in, w_ref, h_ref, g_ref, b_ref, o_ref):
    mix = jnp.dot(o_ref_in[...], w_ref[...], preferred_element_type=_F32)
    o_ref[...] = _layer_norm(ALPHA * h_ref[...] + mix, g_ref[...], b_ref[...])


def _attn_out(o, w_o, h, g, b, tm=512):
    n = h.shape[0]
    vec = pl.BlockSpec((1, D_MODEL), lambda i: (0, 0))
    return pl.pallas_call(
        _attn_out_kernel,
        grid=(n // tm,),
        in_specs=[pl.BlockSpec((tm, SB_WIDTH), lambda i: (i, 0)),
                  pl.BlockSpec((SB_WIDTH, D_MODEL), lambda i: (0, 0)),
                  pl.BlockSpec((tm, D_MODEL), lambda i: (i, 0)),
                  vec, vec],
        out_specs=pl.BlockSpec((tm, D_MODEL), lambda i: (i, 0)),
        out_shape=jax.ShapeDtypeStruct((n, D_MODEL), _F32),
        compiler_params=_params("arbitrary"),
        name="attn_out",
    )(o, w_o, h, g.reshape(1, D_MODEL), b.reshape(1, D_MODEL))


SB_TQ = 128
SB_TK = 256
KV_FRONT = SB_TK - SB_TQ
LKV = LP + KV_FRONT


SB_HPS = 4
SB_LW = SB_HPS * SB_HEAD_DIM
N_QT = LP // SB_TQ
MASKED_SCORE = -1e30
SB_DEAD_LOG2 = 150.0
SB_BOUND_MARGIN = 8.0


def _sb_kernel(q_ref, k_ref, v_ref, tri_ref, o_ref,
               qm_ref, vm_ref, kbm_ref, lb_ref, sp_ref, w_ref, rs_ref, tot_ref):
    tri = tri_ref[...]
    col_minus_row = (lax.broadcasted_iota(jnp.int32, (SB_TQ, SB_TK), 1)
                     - lax.broadcasted_iota(jnp.int32, (SB_TQ, SB_TK), 0))

    def mask_q(c, carry):
        rows = pl.ds(pl.multiple_of(c * SB_TQ, SB_TQ), SB_TQ)
        blk = q_ref[0, rows, :]
        head = lax.broadcasted_iota(jnp.int32, blk.shape, 1) // SB_HEAD_DIM
        for h in range(SB_HPS):
            qm_ref[h, rows, :] = jnp.where(head == h, blk, jnp.zeros_like(blk))
        return carry

    def mask_v(c, carry):
        rows = pl.ds(pl.multiple_of(c * SB_TK, SB_TK), SB_TK)
        blk = v_ref[0, rows, :]
        head = lax.broadcasted_iota(jnp.int32, blk.shape, 1) // SB_HEAD_DIM
        for h in range(SB_HPS):
            vm_ref[h, rows, :] = jnp.where(head == h, blk, jnp.zeros_like(blk))
        kbar = jnp.mean(k_ref[0, rows, :].astype(_F32), axis=0, keepdims=True)
        mean_rows = jnp.broadcast_to(kbar, (LANES, SB_LW))
        row_id = lax.broadcasted_iota(jnp.int32, (LANES, SB_LW), 0)
        row_head = lax.broadcasted_iota(jnp.int32, (LANES, SB_LW), 1) // SB_HEAD_DIM
        kbm_ref[c] = jnp.where(row_id == row_head, mean_rows, 0.0).astype(_BF16)
        return carry

    lax.fori_loop(0, N_QT, mask_q, 0)
    lax.fori_loop(0, LKV // SB_TK, mask_v, 0)

    def stage_a(item, slot):
        qi, kb = item[0], item[1]
        q_rows = pl.ds(pl.multiple_of(qi * SB_TQ, SB_TQ), SB_TQ)
        kblk = k_ref[0, pl.ds(pl.multiple_of(kb * SB_TK, SB_TK), SB_TK), :]
        valid = col_minus_row < qi * SB_TQ + KV_FRONT - kb * SB_TK
        for h in range(SB_HPS):
            z = lax.dot_general(qm_ref[h, q_rows, :], kblk, (((1,), (1,)), ((), ())),
                                preferred_element_type=_F32)
            z = jnp.where(valid, z, MASKED_SCORE)
            neg_abs = pltpu.bitcast(pltpu.bitcast(z, jnp.uint32) | jnp.uint32(0x80000000), _F32)
            s = jnp.maximum(z, 0.0) + jnp.log2(1.0 + jnp.exp2(neg_abs))
            lb_ref[slot, h] = z - s
            sp_ref[slot, h] = s.astype(_BF16)
            rs_ref[slot, h] = jnp.broadcast_to(jnp.sum(s, axis=-1, keepdims=True), (SB_TQ, LANES))

    def stage_b(slot):
        for h in range(SB_HPS):
            later = jnp.dot(sp_ref[slot, h], tri, preferred_element_type=_F32)
            total = jnp.tile(tot_ref[h], (1, SB_TK // LANES))
            w_ref[slot, h] = jnp.exp2(lb_ref[slot, h] - later - total).astype(_BF16)

    def next_totals(item, prev_slot):
        fresh = item[2] == 1
        low = None
        for h in range(SB_HPS):
            tot = jnp.where(fresh, 0.0, tot_ref[h] + rs_ref[prev_slot, h])
            tot_ref[h] = tot
            low = tot if low is None else jnp.minimum(low, tot)
        return jnp.min(low) >= SB_DEAD_LOG2

    def next_block_dead(item):
        qi, kb, first, _ = item
        q = q_ref[0, pl.ds(pl.multiple_of(qi * SB_TQ, SB_TQ), SB_TQ), :]
        zbar = lax.dot_general(q, kbm_ref[kb], (((1,), (1,)), ((), ())),
                               preferred_element_type=_F32)
        zbar = zbar - 0.02 * jnp.abs(zbar) - 0.05
        bound = SB_TK * (jnp.maximum(zbar, 0.0) + jnp.log2(1.0 + jnp.exp2(-jnp.abs(zbar))))
        lane = lax.broadcasted_iota(jnp.int32, bound.shape, 1)
        low = None
        for h in range(SB_HPS):
            cand = tot_ref[h] + jnp.where(lane == h, bound, -MASKED_SCORE)
            low = cand if low is None else jnp.minimum(low, cand)
        return jnp.logical_and(first == 0, jnp.min(low) >= SB_DEAD_LOG2 + SB_BOUND_MARGIN)

    def stage_c(item, slot, acc):
        qi, kb, first, live = item
        k_rows = pl.ds(pl.multiple_of(kb * SB_TK, SB_TK), SB_TK)
        new = jnp.where(first == 1, 0.0, acc)
        for h in range(SB_HPS):
            new = new + jnp.dot(w_ref[slot, h], vm_ref[h, k_rows, :], preferred_element_type=_F32)
        acc = jnp.where(live == 1, new, acc)
        o_ref[0, pl.ds(pl.multiple_of(qi * SB_TQ, SB_TQ), SB_TQ), :] = acc.astype(o_ref.dtype)
        return acc

    def key_blocks(qi):
        return (qi * SB_TQ + SB_TQ + KV_FRONT + SB_TK - 1) // SB_TK

    def advance(item, dead):
        qi, kb, _, live = item
        tile_done = jnp.logical_or(kb == 0, dead)
        nxt = jnp.where(tile_done, qi + 1, qi)
        live = jnp.where(nxt >= N_QT, 0, live)
        nxt = jnp.minimum(nxt, N_QT - 1)
        kb = jnp.where(tile_done, key_blocks(nxt) - 1, kb - 1)
        return nxt, kb, tile_done.astype(jnp.int32), live

    def step(items, slot, acc):
        cur, prev, prev2 = items
        acc = stage_c(prev2, slot, acc)
        stage_b(1 - slot)
        dead = jnp.logical_or(next_totals(cur, 1 - slot), next_block_dead(cur))
        stage_a(cur, slot)
        return (advance(cur, dead), cur, prev), acc

    def four_steps(carry):
        items, acc = carry
        for slot in (0, 1, 0, 1):
            items, acc = step(items, slot, acc)
        return items, acc

    def unfinished(carry):
        (cur, prev, prev2), _ = carry
        return cur[3] + prev[3] + prev2[3] > 0

    as_item = lambda *v: tuple(jnp.int32(x) for x in v)
    item0 = as_item(0, key_blocks(0) - 1, 1, 1)
    item1 = as_item(1, key_blocks(1) - 1, 1, 1)
    tot_ref[...] = jnp.zeros_like(tot_ref)
    stage_a(item0, 0)
    stage_b(0)
    dead = next_totals(item1, 0)
    stage_a(item1, 1)
    carry = ((advance(item1, dead), item1, item0), jnp.zeros((SB_TQ, SB_LW), _F32))
    lax.while_loop(unfinished, four_steps, carry)


def _stick_breaking(q, kv):
    j = jnp.arange(SB_TK)
    tri = (j[:, None] > j[None, :]).astype(_BF16)
    n_steps = SB_WIDTH // SB_LW
    slots = lambda dtype: pltpu.VMEM((2, SB_HPS, SB_TQ, SB_TK), dtype)
    return pl.pallas_call(
        _sb_kernel,
        grid=(BATCH, n_steps),
        in_specs=[pl.BlockSpec((1, LP, SB_LW), lambda b, p: (b, 0, p)),
                  pl.BlockSpec((1, LKV, SB_LW), lambda b, p: (b, 0, p)),
                  pl.BlockSpec((1, LKV, SB_LW), lambda b, p: (b, 0, p + n_steps)),
                  pl.BlockSpec((SB_TK, SB_TK), lambda b, p: (0, 0))],
        out_specs=pl.BlockSpec((1, LP, SB_LW), lambda b, p: (b, 0, p)),
        scratch_shapes=[pltpu.VMEM((SB_HPS, LP, SB_LW), _BF16),
                        pltpu.VMEM((SB_HPS, LKV, SB_LW), _BF16),
                        pltpu.VMEM((LKV // SB_TK, LANES, SB_LW), _BF16),
                        slots(_F32), slots(_BF16), slots(_BF16),
                        pltpu.VMEM((2, SB_HPS, SB_TQ, LANES), _F32),
                        pltpu.VMEM((SB_HPS, SB_TQ, LANES), _F32)],
        out_shape=jax.ShapeDtypeStruct((BATCH, LP, SB_WIDTH), _BF16),
        compiler_params=_params("arbitrary", "arbitrary"),
        name="stick_breaking",
    )(q, kv, kv, tri)


ROUTER_TM = 512


def _router_kernel(h_ref, w_ref, b_ref, tri_ref, cls_ref, rank_ref, cnt_ref, gext_ref, base_ref):
    @pl.when(pl.program_id(0) == 0)
    def _():
        base_ref[...] = jnp.zeros_like(base_ref)

    h = h_ref[...]
    h_hi = h.astype(_BF16)
    h_lo = (h - h_hi.astype(_F32)).astype(_BF16)
    w_hi, w_lo = w_ref[0], w_ref[1]
    nt = lambda a, b: lax.dot_general(a, b, (((1,), (1,)), ((), ())), preferred_element_type=_F32)
    logits = nt(w_hi, h_hi) + nt(w_hi, h_lo) + nt(w_lo, h_hi) + b_ref[...]
    m = jnp.max(logits, axis=0, keepdims=True)
    e = jnp.exp(logits - m)
    p = e / jnp.sum(e, axis=0, keepdims=True)
    ng = N_EXPERT_GROUPS
    pj = [p[j * ng:(j + 1) * ng, :] for j in range(EXPERTS_PER_GROUP)]
    m1 = jnp.maximum(jnp.maximum(pj[0], pj[1]), jnp.maximum(pj[2], pj[3]))
    i1 = jnp.where(pj[0] == m1, 0, jnp.where(pj[1] == m1, 1, jnp.where(pj[2] == m1, 2, 3)))
    qj = [jnp.where(i1 == j, -1.0, pj[j]) for j in range(EXPERTS_PER_GROUP)]
    m2 = jnp.maximum(jnp.maximum(qj[0], qj[1]), jnp.maximum(qj[2], qj[3]))
    i2 = jnp.where(qj[0] == m2, 0, jnp.where(qj[1] == m2, 1, jnp.where(qj[2] == m2, 2, 3)))
    score = m1 + m2
    gid = lax.broadcasted_iota(jnp.int32, score.shape, 0)
    best = jnp.max(score, axis=0, keepdims=True)
    gsel = jnp.min(jnp.where(score == best, gid, ng), axis=0, keepdims=True)
    chosen = gid == gsel
    pick_f = lambda v: jnp.sum(jnp.where(chosen, v, 0.0), axis=0, keepdims=True)
    pick_i = lambda v: jnp.sum(jnp.where(chosen, v, 0), axis=0, keepdims=True)
    v1, v2, j1, j2 = pick_f(m1), pick_f(m2), pick_i(i1), pick_i(i2)
    denom = v1 + v2
    g1 = v1 / denom
    g2 = v2 / denom
    first_low = j1 < j2
    lo = jnp.where(first_low, j1, j2)
    hi = jnp.where(first_low, j2, j1)
    base = jnp.where(lo == 0, 0, jnp.where(lo == 1, 3, 5))
    cls = gsel * N_PAIR_CLASSES + base + hi - lo - 1
    cls_ref[...] = cls

    member = lax.broadcasted_iota(jnp.int32, (CLS_ROWS, ROUTER_TM), 0) == cls
    onehot = jnp.where(member, 1.0, 0.0)
    upto = jnp.dot(onehot.astype(_BF16), tri_ref[...], preferred_element_type=_F32)
    before = base_ref[...]
    rank = jnp.sum(jnp.where(member, upto - 1.0 + before, 0.0), axis=0, keepdims=True)
    rank_ref[...] = rank.astype(jnp.int32)
    after = before + jnp.sum(onehot, axis=1, keepdims=True)
    base_ref[...] = after
    cnt_ref[...] = after[:, :LANES]

    gates = jnp.concatenate([jnp.where(first_low, g1, g2), jnp.where(first_low, g2, g1),
                             jnp.zeros((LANES - 2, ROUTER_TM), _F32)], axis=0)
    gext_ref[...] = gates.T


def _router(h, router_w, router_b):
    n = h.shape[0]
    perm = jnp.array([g * EXPERTS_PER_GROUP + j for j in range(EXPERTS_PER_GROUP)
                      for g in range(N_EXPERT_GROUPS)])
    w_t = router_w.astype(_F32).T[perm]
    w_hi = w_t.astype(_BF16)
    w_t = jnp.stack([w_hi, (w_t - w_hi.astype(_F32)).astype(_BF16)])
    b_t = router_b.astype(_F32)[perm].reshape(N_EXPERTS, 1)
    j = jnp.arange(ROUTER_TM)
    tri = (j[:, None] <= j[None, :]).astype(_BF16)
    out = pl.BlockSpec((1, ROUTER_TM), lambda i: (0, i))
    return pl.pallas_call(
        _router_kernel,
        grid=(n // ROUTER_TM,),
        in_specs=[pl.BlockSpec((ROUTER_TM, D_MODEL), lambda i: (i, 0)),
                  pl.BlockSpec((2, N_EXPERTS, D_MODEL), lambda i: (0, 0, 0)),
                  pl.BlockSpec((N_EXPERTS, 1), lambda i: (0, 0)),
                  pl.BlockSpec((ROUTER_TM, ROUTER_TM), lambda i: (0, 0))],
        out_specs=[out, out,
                   pl.BlockSpec((CLS_ROWS, LANES), lambda i: (0, 0)),
                   pl.BlockSpec((ROUTER_TM, LANES), lambda i: (i, 0))],
        out_shape=[jax.ShapeDtypeStruct((1, n), jnp.int32),
                   jax.ShapeDtypeStruct((1, n), jnp.int32),
                   jax.ShapeDtypeStruct((CLS_ROWS, LANES), _F32),
                   jax.ShapeDtypeStruct((n, LANES), _F32)],
        scratch_shapes=[pltpu.VMEM((CLS_ROWS, ROUTER_TM), _F32)],
        compiler_params=_params("arbitrary"),
        name="router",
    )(h, w_t, b_t, tri)


def _class_experts():
    lo, hi = [], []
    for g in range(N_EXPERT_GROUPS):
        for a in range(EXPERTS_PER_GROUP):
            for b in range(a + 1, EXPERTS_PER_GROUP):
                lo.append(g * EXPERTS_PER_GROUP + a)
                hi.append(g * EXPERTS_PER_GROUP + b)
    return jnp.array(lo, jnp.int32), jnp.array(hi, jnp.int32)


def _expert_kernel(elo_ref, ehi_ref, nused_ref, x_ref, wg1, wu1, wd1, wg2, wu2, wd2, o_ref):
    t = pl.program_id(0)

    @pl.when(t < nused_ref[0])
    def _():
        x = x_ref[:, :D_MODEL].astype(_BF16)
        gates = x_ref[:, D_MODEL:]

        def ffn(wg, wu, wd, gate):
            a = jnp.dot(x, wg[0], preferred_element_type=_F32)
            u = jnp.dot(x, wu[0], preferred_element_type=_F32)
            hid = (a * _sigmoid(a)) * u * gate
            return jnp.dot(hid.astype(_BF16), wd[0], preferred_element_type=_F32)

        o_ref[...] = ffn(wg1, wu1, wd1, gates[:, 0:1]) + ffn(wg2, wu2, wd2, gates[:, 1:2])

    @pl.when(t >= nused_ref[0])
    def _():
        o_ref[...] = jnp.zeros_like(o_ref)


def _experts(x_sorted, tile_lo, tile_hi, n_used, w_gate, w_up, w_down):
    up_spec = lambda which: pl.BlockSpec(
        (1, D_MODEL, D_EXPERT), lambda t, lo, hi, nu: ((lo, hi)[which][t], 0, 0))
    down_spec = lambda which: pl.BlockSpec(
        (1, D_EXPERT, D_MODEL), lambda t, lo, hi, nu: ((lo, hi)[which][t], 0, 0))
    row = lambda d: pl.BlockSpec((MOE_TM, d), lambda t, lo, hi, nu: (t, 0))
    grid_spec = pltpu.PrefetchScalarGridSpec(
        num_scalar_prefetch=3,
        grid=(MOE_TILES,),
        in_specs=[row(XROW), up_spec(0), up_spec(0), down_spec(0),
                  up_spec(1), up_spec(1), down_spec(1)],
        out_specs=row(D_MODEL),
    )
    return pl.pallas_call(
        _expert_kernel,
        grid_spec=grid_spec,
        out_shape=jax.ShapeDtypeStruct((MOE_ROWS, D_MODEL), _F32),
        compiler_params=_params("arbitrary"),
        name="experts",
    )(tile_lo, tile_hi, n_used, x_sorted, w_gate, w_up, w_down, w_gate, w_up, w_down)


SUB = 8


def _tiles(x):
    return x.reshape(x.shape[0] // SUB, SUB, x.shape[1])


def _issue_rows(row_copy, pos_ref, base, slot):
    def issue(g, carry):
        for u in range(SUB):
            p = pos_ref[base + g * SUB + u]
            row_copy(slot, g, u, lax.shift_right_logical(p, 3), p & (SUB - 1)).start(priority=u % 2)
        return carry
    lax.fori_loop(0, ROW_TM // SUB, issue, 0)


def _dispatch_kernel(pos_ref, h_ref, g_ref, init_ref, out_ref, stage_ref, sem):
    del init_ref
    s = pl.program_id(0)
    last = pl.num_programs(0) - 1
    slot = s % 2

    def row_copy(slot, g, u, tile, row):
        return pltpu.make_async_copy(stage_ref.at[slot, g, pl.ds(u, 1), :],
                                     out_ref.at[tile, pl.ds(row, 1), :], sem.at[slot])

    def wait_slot(slot):
        pltpu.make_async_copy(stage_ref.at[slot], out_ref.at[pl.ds(0, ROW_TM // SUB)],
                              sem.at[slot]).wait()

    @pl.when(s >= 2)
    def _():
        wait_slot(slot)

    stage_ref[slot, :, :, :D_MODEL] = h_ref[...]
    stage_ref[slot, :, :, D_MODEL:] = g_ref[...]
    _issue_rows(row_copy, pos_ref, s * ROW_TM, slot)

    @pl.when(s == last)
    def _():
        wait_slot(slot)
        wait_slot(1 - slot)


def _dispatch(pos, h, gext):
    n = h.shape[0]
    assert n // ROW_TM >= 2
    tile_rows = ROW_TM // SUB
    grid_spec = pltpu.PrefetchScalarGridSpec(
        num_scalar_prefetch=1,
        grid=(n // ROW_TM,),
        in_specs=[pl.BlockSpec((tile_rows, SUB, D_MODEL), lambda i, pos: (i, 0, 0)),
                  pl.BlockSpec((tile_rows, SUB, LANES), lambda i, pos: (i, 0, 0)),
                  pl.BlockSpec(memory_space=pl.ANY)],
        out_specs=pl.BlockSpec(memory_space=pl.ANY),
        scratch_shapes=[pltpu.VMEM((2, tile_rows, SUB, XROW), _F32), pltpu.SemaphoreType.DMA((2,))],
    )
    out = pl.pallas_call(
        _dispatch_kernel,
        grid_spec=grid_spec,
        out_shape=jax.ShapeDtypeStruct((MOE_ROWS // SUB, SUB, XROW), _F32),
        input_output_aliases={3: 0},
        compiler_params=_params("arbitrary"),
        name="dispatch",
    )(pos, _tiles(h), _tiles(gext), jnp.zeros((MOE_ROWS // SUB, SUB, XROW), _F32))
    return out.reshape(MOE_ROWS, XROW)


def _combine_kernel(pos_ref, h_ref, y_ref, g_ref, b_ref, o_ref, buf_ref, sem):
    s = pl.program_id(0)
    last = pl.num_programs(0) - 1
    slot = s % 2

    def row_copy(slot, g, u, tile, row):
        return pltpu.make_async_copy(y_ref.at[tile, pl.ds(row, 1), :],
                                     buf_ref.at[slot, g, pl.ds(u, 1), :], sem.at[slot])

    @pl.when(s == 0)
    def _():
        _issue_rows(row_copy, pos_ref, 0, 0)

    @pl.when(s < last)
    def _():
        _issue_rows(row_copy, pos_ref, (s + 1) * ROW_TM, 1 - slot)

    pltpu.make_async_copy(y_ref.at[pl.ds(0, ROW_TM // SUB)], buf_ref.at[slot], sem.at[slot]).wait()
    o_ref[...] = _layer_norm(ALPHA * h_ref[...] + buf_ref[slot], g_ref[...], b_ref[...])


def _combine(pos, h, y_sorted, g, b):
    n = h.shape[0]
    tile_rows = ROW_TM // SUB
    row = pl.BlockSpec((tile_rows, SUB, D_MODEL), lambda i, pos: (i, 0, 0))
    vec = pl.BlockSpec((1, D_MODEL), lambda i, pos: (0, 0))
    grid_spec = pltpu.PrefetchScalarGridSpec(
        num_scalar_prefetch=1,
        grid=(n // ROW_TM,),
        in_specs=[row, pl.BlockSpec(memory_space=pl.ANY), vec, vec],
        out_specs=row,
        scratch_shapes=[pltpu.VMEM((2, tile_rows, SUB, D_MODEL), _F32),
                        pltpu.SemaphoreType.DMA((2,))],
    )
    out = pl.pallas_call(
        _combine_kernel,
        grid_spec=grid_spec,
        out_shape=jax.ShapeDtypeStruct((n // SUB, SUB, D_MODEL), _F32),
        compiler_params=_params("arbitrary"),
        name="combine",
    )(pos, _tiles(h), _tiles(y_sorted), g.reshape(1, D_MODEL), b.reshape(1, D_MODEL))
    return out.reshape(n, D_MODEL)


def _moe_layer(h, router_w, router_b, w_gate, w_up, w_down, ln_g, ln_b):
    cls, rank, cnt, gext = _router(h, router_w, router_b)
    counts = cnt[:N_CLASSES, 0].astype(jnp.int32)
    tiles = (counts + MOE_TM - 1) // MOE_TM
    tile_end = jnp.cumsum(tiles)
    class_start = (tile_end - tiles) * MOE_TM
    classes = jnp.arange(N_CLASSES, dtype=jnp.int32)
    pos = jnp.sum(jnp.where(cls[0][:, None] == classes[None, :], class_start[None, :], 0),
                  axis=1) + rank[0]
    n_used = tile_end[-1]
    tile_ids = jnp.minimum(jnp.arange(MOE_TILES, dtype=jnp.int32), n_used - 1)
    tile_cls = jnp.sum((tile_ids[:, None] >= tile_end[None, :]).astype(jnp.int32), axis=1)
    class_lo, class_hi = _class_experts()
    x_sorted = _dispatch(pos, h, gext)
    y_sorted = _experts(x_sorted, class_lo[tile_cls], class_hi[tile_cls],
                        n_used.reshape(1).astype(jnp.int32), w_gate, w_up, w_down)
    return _combine(pos, h, y_sorted, ln_g, ln_b)


def kernel(x, meta_tokens, mamba_w_in, mamba_conv_w, mamba_conv_b, mamba_dt_bias, mamba_a_log,
           mamba_d_skip, mamba_norm_w, mamba_w_out, sb_w_q, sb_w_o, shared_w_k, shared_w_v,
           ln_mix_g, ln_mix_b, ln_ffn_g, ln_ffn_b, router_w, router_b, moe_w_gate, moe_w_up,
           moe_w_down):
    bsz = x.shape[0]
    meta = jnp.broadcast_to(meta_tokens.astype(x.dtype)[None], (bsz, N_META, D_MODEL))
    h = jnp.concatenate([jnp.zeros((bsz, PAD, D_MODEL), x.dtype), meta, x], axis=1)
    h = h.reshape(NP, D_MODEL)
    kv = None
    for layer in range(DEPTH):
        if layer < N_A_LAYERS:
            w_in = mamba_w_in[layer].astype(_BF16)
            w_dt = jnp.pad(w_in[:, ZX_DIM:], ((0, 0), (0, LANES - SSM_HEADS)))
            proj = _in_proj(h, w_in, mamba_conv_w[layer], mamba_conv_b[layer])
            dt_raw = _matmul(h, w_dt, LANES, 512, LANES, _F32)
            y = _ssd(proj.reshape(BATCH, LP, ZX_DIM), dt_raw.reshape(BATCH, LP, LANES),
                     mamba_dt_bias[layer], mamba_a_log[layer], mamba_d_skip[layer])
            h = _mamba_out(y.reshape(NP, D_INNER), proj, mamba_norm_w[layer],
                           mamba_w_out[layer].astype(_BF16), h, ln_mix_g[layer], ln_mix_b[layer])
        else:
            j = layer - N_A_LAYERS
            if kv is None:
                w_kv = jnp.concatenate([shared_w_k, shared_w_v], axis=1).astype(_BF16)
                kv = _matmul(h, w_kv, 2 * SB_WIDTH, 512, 2 * SB_WIDTH, _BF16).reshape(BATCH, LP, 2 * SB_WIDTH)
                kv = jnp.pad(kv[:, PAD:], ((0, 0), (PAD + KV_FRONT, 0), (0, 0)))
            q = _matmul(h, sb_w_q[j].astype(_BF16), SB_WIDTH, 512, 1024, _BF16,
                        scale=SB_SCALE * LOG2_E)
            o = _stick_breaking(q.reshape(BATCH, LP, SB_WIDTH), kv)
            h = _attn_out(o.reshape(NP, SB_WIDTH), sb_w_o[j].astype(_BF16), h,
                          ln_mix_g[layer], ln_mix_b[layer])
        h = _moe_layer(h, router_w, router_b, moe_w_gate[layer].astype(_BF16),
                       moe_w_up[layer].astype(_BF16), moe_w_down[layer].astype(_BF16),
                       ln_ffn_g[layer], ln_ffn_b[layer])
    return h.reshape(bsz, LP, D_MODEL)[:, PAD + N_META:]
```

```python
import functools
import math

import jax
import jax.numpy as jnp
from jax import lax
from jax.experimental import pallas as pl
from jax.experimental.pallas import tpu as pltpu

D_MODEL = 1024
BATCH = 8
SEQ = 2048
DEPTH = 4
N_META = 16
N_A_LAYERS = DEPTH // 2
ALPHA = (2.0 * DEPTH) ** 0.25
LN_EPS = 1e-5
D_INNER = 2048
SSM_HEAD_DIM = 64
SSM_HEADS = 32
SSM_GROUPS = 4
D_STATE = 128
CONV_WIDTH = 4
CHUNK = 128
CONV_DIM = D_INNER + 2 * SSM_GROUPS * D_STATE
SB_HEADS = 16
SB_HEAD_DIM = 64
SB_WIDTH = SB_HEADS * SB_HEAD_DIM
SB_SCALE = SB_HEAD_DIM ** -0.5
LOG2_E = math.log2(math.e)
N_EXPERTS = 16
N_EXPERT_GROUPS = 4
EXPERTS_PER_GROUP = 4
D_EXPERT = 512

L_REAL = N_META + SEQ
PAD = (-L_REAL) % CHUNK
LP = L_REAL + PAD
N_CHUNKS = LP // CHUNK
NP = BATCH * LP
LANES = 128
N_PAIR_CLASSES = 6
N_CLASSES = N_EXPERT_GROUPS * N_PAIR_CLASSES
MOE_TM = 256
MOE_TILES = NP // MOE_TM + N_CLASSES
MOE_ROWS = MOE_TILES * MOE_TM
CLS_ROWS = 32
XROW = D_MODEL + LANES
ROW_TM = 512
VMEM_LIMIT = 48 * 1024 * 1024

_F32 = jnp.float32
_BF16 = jnp.bfloat16


def _params(*sem):
    return pltpu.CompilerParams(dimension_semantics=sem, vmem_limit_bytes=VMEM_LIMIT)


def _sigmoid(x):
    return 1.0 / (1.0 + jnp.exp(-x))


def _softplus(x):
    return jnp.maximum(x, 0.0) + jnp.log(1.0 + jnp.exp(-jnp.abs(x)))


def _layer_norm(t, g, b):
    mu = jnp.mean(t, axis=-1, keepdims=True)
    d = t - mu
    var = jnp.mean(d * d, axis=-1, keepdims=True)
    return d * lax.rsqrt(var + LN_EPS) * g + b


def _mm_kernel(x_ref, w_ref, o_ref, *, scale):
    x = x_ref[...].astype(_BF16)
    acc = jnp.dot(x, w_ref[...], preferred_element_type=_F32)
    if scale != 1.0:
        acc = acc * scale
    o_ref[...] = acc.astype(o_ref.dtype)


def _matmul(x, w, n_out, tm, tn, out_dtype, scale=1.0):
    m, k = x.shape
    return pl.pallas_call(
        functools.partial(_mm_kernel, scale=scale),
        grid=(n_out // tn, m // tm),
        in_specs=[pl.BlockSpec((tm, k), lambda j, i: (i, 0)),
                  pl.BlockSpec((k, tn), lambda j, i: (0, j))],
        out_specs=pl.BlockSpec((tm, tn), lambda j, i: (i, j)),
        out_shape=jax.ShapeDtypeStruct((m, n_out), out_dtype),
        compiler_params=_params("arbitrary", "arbitrary"),
        name="matmul",
    )(x, w)


def _add_ln_kernel(h_ref, m_ref, g_ref, b_ref, o_ref):
    t = ALPHA * h_ref[...] + m_ref[...]
    o_ref[...] = _layer_norm(t, g_ref[...], b_ref[...])


def _add_ln(h, mix, g, b, tm=512):
    n, d = h.shape
    row = pl.BlockSpec((tm, d), lambda i: (i, 0))
    vec = pl.BlockSpec((1, d), lambda i: (0, 0))
    return pl.pallas_call(
        _add_ln_kernel,
        grid=(n // tm,),
        in_specs=[row, row, vec, vec],
        out_specs=row,
        out_shape=jax.ShapeDtypeStruct((n, d), _F32),
        compiler_params=_params("arbitrary"),
        name="add_ln",
    )(h, mix, g.reshape(1, d), b.reshape(1, d))


ZX_DIM = D_INNER + CONV_DIM
PROJ_TM = LP // 4
PROJ_TN = ZX_DIM // 2
PROJ_CHUNK = 512
CONV_KEEP = 8


def _in_proj_kernel(x_ref, w_ref, cw_ref, cb_ref, o_ref, hist_ref):
    j = pl.program_id(0)
    seq_start = pl.program_id(1) % (LP // PROJ_TM) == 0
    x = x_ref[...].astype(_BF16)
    rows = lax.broadcasted_iota(jnp.int32, (PROJ_TM, PROJ_CHUNK), 0)
    real = jnp.logical_or(jnp.logical_not(seq_start), rows >= PAD)

    def plain(c):
        sl = slice(c * PROJ_CHUNK, (c + 1) * PROJ_CHUNK)
        o_ref[:, sl] = jnp.dot(x, w_ref[:, sl], preferred_element_type=_F32).astype(o_ref.dtype)

    def conv(c):
        sl = slice(c * PROJ_CHUNK, (c + 1) * PROJ_CHUNK)
        cur = jnp.where(real, jnp.dot(x, w_ref[:, sl], preferred_element_type=_F32), 0.0)
        prev = jnp.where(seq_start, 0.0, hist_ref[c])
        hist_ref[c] = cur[PROJ_TM - CONV_KEEP:, :]
        xx = jnp.concatenate([prev, cur], axis=0)
        w = cw_ref[:, sl]
        acc = cb_ref[:, sl] + w[CONV_WIDTH - 1:CONV_WIDTH, :] * cur
        for back in range(1, CONV_WIDTH):
            k = CONV_WIDTH - 1 - back
            acc = acc + w[k:k + 1, :] * pltpu.roll(xx, back, 0)[CONV_KEEP:, :]
        o_ref[:, sl] = (acc * _sigmoid(acc)).astype(o_ref.dtype)

    for block in range(ZX_DIM // PROJ_TN):
        @pl.when(j == block)
        def _(block=block):
            for c in range(PROJ_TN // PROJ_CHUNK):
                if block * PROJ_TN + c * PROJ_CHUNK < D_INNER:
                    plain(c)
                else:
                    conv(c)


def _in_proj(h, w_in, conv_w, conv_b):
    n, k = h.shape
    cw = jnp.pad(conv_w.astype(_F32), ((0, 0), (D_INNER, 0)))
    cb = jnp.pad(conv_b.astype(_F32), (D_INNER, 0)).reshape(1, ZX_DIM)
    return pl.pallas_call(
        _in_proj_kernel,
        grid=(ZX_DIM // PROJ_TN, n // PROJ_TM),
        in_specs=[pl.BlockSpec((PROJ_TM, k), lambda j, i: (i, 0)),
                  pl.BlockSpec((k, PROJ_TN), lambda j, i: (0, j)),
                  pl.BlockSpec((CONV_WIDTH, PROJ_TN), lambda j, i: (0, j)),
                  pl.BlockSpec((1, PROJ_TN), lambda j, i: (0, j))],
        out_specs=pl.BlockSpec((PROJ_TM, PROJ_TN), lambda j, i: (i, j)),
        out_shape=jax.ShapeDtypeStruct((n, ZX_DIM), _BF16),
        scratch_shapes=[pltpu.VMEM((PROJ_TN // PROJ_CHUNK, CONV_KEEP, PROJ_CHUNK), _F32)],
        compiler_params=_params("arbitrary", "arbitrary"),
        name="in_proj",
    )(h, w_in, cw, cb)


N_HEAD_PAIRS = SSM_HEADS // 2
PAIRS_PER_GROUP = N_HEAD_PAIRS // SSM_GROUPS


def _ssd_kernel(xs_ref, b_ref, c_ref, h_ref, wdt_ref, dtb_ref, aneg_ref, dskip_ref, ltri_ref,
                exp_ref, y_ref, state_ref):
    c = pl.program_id(1)

    @pl.when(c == 0)
    def _():
        state_ref[...] = jnp.zeros_like(state_ref)

    rows = lax.broadcasted_iota(jnp.int32, (CHUNK, LANES), 0)
    cols = lax.broadcasted_iota(jnp.int32, (CHUNK, LANES), 1)
    causal = cols <= rows
    left = cols < SSM_HEAD_DIM

    dt_raw = jnp.dot(h_ref[0].astype(_BF16), wdt_ref[...], preferred_element_type=_F32)
    dt = _softplus(dt_raw + dtb_ref[...])
    dt = jnp.where(rows + c * CHUNK >= PAD, dt, 0.0)
    a = dt * aneg_ref[...]
    a_cum = jnp.dot(ltri_ref[...], a, preferred_element_type=_F32,
                    precision=lax.Precision.HIGHEST)
    a_cum_t = a_cum.T
    total = a_cum[CHUNK - 1:CHUNK, :]
    to_end = jnp.exp(total - a_cum)
    chunk_decay = jnp.exp(jnp.broadcast_to(total, (8, LANES)))

    expand = exp_ref[...]
    dt_x = jnp.dot(dt.astype(_BF16), expand, preferred_element_type=_F32)
    to_end_x = jnp.dot(to_end.astype(_BF16), expand, preferred_element_type=_F32)
    chunk_decay_x = jnp.dot(chunk_decay.astype(_BF16), expand, preferred_element_type=_F32)

    for g in range(SSM_GROUPS):
        bg = b_ref[0, :, g * D_STATE:(g + 1) * D_STATE]
        cg = c_ref[0, :, g * D_STATE:(g + 1) * D_STATE]
        cb = lax.dot_general(cg, bg, (((1,), (1,)), ((), ())), preferred_element_type=_F32)
        bg_t = bg.astype(_F32).T.astype(_BF16)
        cg32 = cg.astype(_F32)
        for jj in range(PAIRS_PER_GROUP):
            j = g * PAIRS_PER_GROUP + jj
            sl = slice(j * LANES, (j + 1) * LANES)
            x2 = xs_ref[0, :, sl].astype(_F32)
            xdt = x2 * dt_x[:, sl]
            parts, from_start = [], []
            for h in (2 * j, 2 * j + 1):
                a_col = jnp.broadcast_to(a_cum[:, h:h + 1], (CHUNK, CHUNK))
                decay = jnp.where(causal, jnp.exp(a_col - a_cum_t[h:h + 1, :]), 0.0)
                parts.append((cb * decay).astype(_BF16))
                from_start.append(jnp.exp(a_col))
            for fs in from_start:
                parts.append((cg32 * fs).astype(_BF16))
            lhs = jnp.concatenate(parts, axis=1)
            state = state_ref[j]
            xb = xdt.astype(_BF16)
            sb = state.astype(_BF16)
            zero = jnp.zeros_like(xb)
            rhs = jnp.concatenate([jnp.where(left, xb, zero), jnp.where(left, zero, xb),
                                   jnp.where(left, sb, zero), jnp.where(left, zero, sb)], axis=0)
            y2 = jnp.dot(lhs, rhs, preferred_element_type=_F32)
            y2 = y2 + dskip_ref[:, sl] * x2
            y_ref[0, :, sl] = y2.astype(y_ref.dtype)
            upd = jnp.dot(bg_t, (xdt * to_end_x[:, sl]).astype(_BF16), preferred_element_type=_F32)
            state_ref[j] = state * chunk_decay_x[0:1, sl] + upd


def _ssd(proj, h, w_dt, dt_bias, a_log, d_skip):
    pad_h = LANES - SSM_HEADS
    dtb = jnp.pad(dt_bias.astype(_F32), (0, pad_h)).reshape(1, LANES)
    aneg = jnp.pad(-jnp.exp(a_log.astype(_F32)), (0, pad_h)).reshape(1, LANES)
    dskip = jnp.repeat(d_skip.astype(_F32), SSM_HEAD_DIM).reshape(1, D_INNER)
    ltri = jnp.tril(jnp.ones((CHUNK, CHUNK), _F32))
    expand = (jnp.arange(LANES)[:, None] == (jnp.arange(D_INNER)[None, :] // SSM_HEAD_DIM)).astype(_BF16)
    gn = SSM_GROUPS * D_STATE
    vec = lambda n: pl.BlockSpec((1, n), lambda b, c: (0, 0))
    return pl.pallas_call(
        _ssd_kernel,
        grid=(BATCH, N_CHUNKS),
        in_specs=[pl.BlockSpec((1, CHUNK, D_INNER), lambda b, c: (b, c, 1)),
                  pl.BlockSpec((1, CHUNK, gn), lambda b, c: (b, c, 2 * D_INNER // gn)),
                  pl.BlockSpec((1, CHUNK, gn), lambda b, c: (b, c, 2 * D_INNER // gn + 1)),
                  pl.BlockSpec((1, CHUNK, D_MODEL), lambda b, c: (b, c, 0)),
                  pl.BlockSpec((D_MODEL, LANES), lambda b, c: (0, 0)),
                  vec(LANES), vec(LANES), vec(D_INNER),
                  pl.BlockSpec((CHUNK, CHUNK), lambda b, c: (0, 0)),
                  pl.BlockSpec((LANES, D_INNER), lambda b, c: (0, 0))],
        out_specs=pl.BlockSpec((1, CHUNK, D_INNER), lambda b, c: (b, c, 0)),
        out_shape=jax.ShapeDtypeStruct((BATCH, LP, D_INNER), _BF16),
        scratch_shapes=[pltpu.VMEM((N_HEAD_PAIRS, D_STATE, LANES), _F32)],
        compiler_params=_params("arbitrary", "arbitrary"),
        name="ssd",
    )(proj, proj, proj, h, w_dt, dtb, aneg, dskip, ltri, expand)


def _mamba_out_kernel(y_ref, z_ref, nw_ref, w_ref, h_ref, g_ref, b_ref, o_ref):
    y = y_ref[...].astype(_F32)
    z = z_ref[...].astype(_F32)
    yg = y * (z * _sigmoid(z))
    gw = D_INNER // SSM_GROUPS
    parts = []
    for g in range(SSM_GROUPS):
        blk = yg[:, g * gw:(g + 1) * gw]
        ms = jnp.mean(blk * blk, axis=-1, keepdims=True)
        parts.append(blk * lax.rsqrt(ms + LN_EPS))
    yn = jnp.concatenate(parts, axis=1) * nw_ref[...]
    mix = jnp.dot(yn.astype(_BF16), w_ref[...], preferred_element_type=_F32)
    o_ref[...] = _layer_norm(ALPHA * h_ref[...] + mix, g_ref[...], b_ref[...])


def _mamba_out(y, proj, norm_w, w_out, h, g, b, tm=256):
    n = h.shape[0]
    vec = lambda d: pl.BlockSpec((1, d), lambda i: (0, 0))
    return pl.pallas_call(
        _mamba_out_kernel,
        grid=(n // tm,),
        in_specs=[pl.BlockSpec((tm, D_INNER), lambda i: (i, 0)),
                  pl.BlockSpec((tm, D_INNER), lambda i: (i, 0)),
                  vec(D_INNER),
                  pl.BlockSpec((D_INNER, D_MODEL), lambda i: (0, 0)),
                  pl.BlockSpec((tm, D_MODEL), lambda i: (i, 0)),
                  vec(D_MODEL), vec(D_MODEL)],
        out_specs=pl.BlockSpec((tm, D_MODEL), lambda i: (i, 0)),
        out_shape=jax.ShapeDtypeStruct((n, D_MODEL), _F32),
        compiler_params=_params("arbitrary"),
        name="mamba_out",
    )(y, proj, norm_w.reshape(1, D_INNER).astype(_F32), w_out, h,
      g.reshape(1, D_MODEL), b.reshape(1, D_MODEL))


def _attn_out_kernel(o_ref_in, w_ref, h_ref, g_ref, b_ref, o_ref):
    mix = jnp.dot(o_ref_in[...], w_ref[...], preferred_element_type=_F32)
    o_ref[...] = _layer_norm(ALPHA * h_ref[...] + mix, g_ref[...], b_ref[...])


def _attn_out(o, w_o, h, g, b, tm=512):
    n = h.shape[0]
    vec = pl.BlockSpec((1, D_MODEL), lambda i: (0, 0))
    return pl.pallas_call(
        _attn_out_kernel,
        grid=(n // tm,),
        in_specs=[pl.BlockSpec((tm, SB_WIDTH), lambda i: (i, 0)),
                  pl.BlockSpec((SB_WIDTH, D_MODEL), lambda i: (0, 0)),
                  pl.BlockSpec((tm, D_MODEL), lambda i: (i, 0)),
                  vec, vec],
        out_specs=pl.BlockSpec((tm, D_MODEL), lambda i: (i, 0)),
        out_shape=jax.ShapeDtypeStruct((n, D_MODEL), _F32),
        compiler_params=_params("arbitrary"),
        name="attn_out",
    )(o, w_o, h, g.reshape(1, D_MODEL), b.reshape(1, D_MODEL))


SB_TQ = 128
SB_TK = 256
KV_FRONT = SB_TK - SB_TQ
LKV = LP + KV_FRONT


SB_HPS = 4
SB_LW = SB_HPS * SB_HEAD_DIM
N_QT = LP // SB_TQ
MASKED_SCORE = -1e30
SB_DEAD_LOG2 = 150.0
SB_BOUND_MARGIN = 8.0


def _sb_kernel(q_ref, k_ref, v_ref, tri_ref, o_ref,
               qm_ref, vm_ref, kbm_ref, lb_ref, sp_ref, w_ref, rs_ref, tot_ref):
    tri = tri_ref[...]
    col_minus_row = (lax.broadcasted_iota(jnp.int32, (SB_TQ, SB_TK), 1)
                     - lax.broadcasted_iota(jnp.int32, (SB_TQ, SB_TK), 0))

    def mask_q(c, carry):
        rows = pl.ds(pl.multiple_of(c * SB_TQ, SB_TQ), SB_TQ)
        blk = q_ref[0, rows, :]
        head = lax.broadcasted_iota(jnp.int32, blk.shape, 1) // SB_HEAD_DIM
        for h in range(SB_HPS):
            qm_ref[h, rows, :] = jnp.where(head == h, blk, jnp.zeros_like(blk))
        return carry

    def mask_v(c, carry):
        rows = pl.ds(pl.multiple_of(c * SB_TK, SB_TK), SB_TK)
        blk = v_ref[0, rows, :]
        head = lax.broadcasted_iota(jnp.int32, blk.shape, 1) // SB_HEAD_DIM
        for h in range(SB_HPS):
            vm_ref[h, rows, :] = jnp.where(head == h, blk, jnp.zeros_like(blk))
        kbar = jnp.mean(k_ref[0, rows, :].astype(_F32), axis=0, keepdims=True)
        mean_rows = jnp.broadcast_to(kbar, (LANES, SB_LW))
        row_id = lax.broadcasted_iota(jnp.int32, (LANES, SB_LW), 0)
        row_head = lax.broadcasted_iota(jnp.int32, (LANES, SB_LW), 1) // SB_HEAD_DIM
        kbm_ref[c] = jnp.where(row_id == row_head, mean_rows, 0.0).astype(_BF16)
        return carry

    lax.fori_loop(0, N_QT, mask_q, 0)
    lax.fori_loop(0, LKV // SB_TK, mask_v, 0)

    def stage_a(item, slot):
        qi, kb = item[0], item[1]
        q_rows = pl.ds(pl.multiple_of(qi * SB_TQ, SB_TQ), SB_TQ)
        kblk = k_ref[0, pl.ds(pl.multiple_of(kb * SB_TK, SB_TK), SB_TK), :]
        valid = col_minus_row < qi * SB_TQ + KV_FRONT - kb * SB_TK
        for h in range(SB_HPS):
            z = lax.dot_general(qm_ref[h, q_rows, :], kblk, (((1,), (1,)), ((), ())),
                                preferred_element_type=_F32)
            z = jnp.where(valid, z, MASKED_SCORE)
            neg_abs = pltpu.bitcast(pltpu.bitcast(z, jnp.uint32) | jnp.uint32(0x80000000), _F32)
            s = jnp.maximum(z, 0.0) + jnp.log2(1.0 + jnp.exp2(neg_abs))
            lb_ref[slot, h] = z - s
            sp_ref[slot, h] = s.astype(_BF16)
            rs_ref[slot, h] = jnp.broadcast_to(jnp.sum(s, axis=-1, keepdims=True), (SB_TQ, LANES))

    def stage_b(slot):
        for h in range(SB_HPS):
            later = jnp.dot(sp_ref[slot, h], tri, preferred_element_type=_F32)
            total = jnp.tile(tot_ref[h], (1, SB_TK // LANES))
            w_ref[slot, h] = jnp.exp2(lb_ref[slot, h] - later - total).astype(_BF16)

    def next_totals(item, prev_slot):
        fresh = item[2] == 1
        low = None
        for h in range(SB_HPS):
            tot = jnp.where(fresh, 0.0, tot_ref[h] + rs_ref[prev_slot, h])
            tot_ref[h] = tot
            low = tot if low is None else jnp.minimum(low, tot)
        return jnp.min(low) >= SB_DEAD_LOG2

    def next_block_dead(item):
        qi, kb, first, _ = item
        q = q_ref[0, pl.ds(pl.multiple_of(qi * SB_TQ, SB_TQ), SB_TQ), :]
        zbar = lax.dot_general(q, kbm_ref[kb], (((1,), (1,)), ((), ())),
                               preferred_element_type=_F32)
        zbar = zbar - 0.02 * jnp.abs(zbar) - 0.05
        bound = SB_TK * (jnp.maximum(zbar, 0.0) + jnp.log2(1.0 + jnp.exp2(-jnp.abs(zbar))))
        lane = lax.broadcasted_iota(jnp.int32, bound.shape, 1)
        low = None
        for h in range(SB_HPS):
            cand = tot_ref[h] + jnp.where(lane == h, bound, -MASKED_SCORE)
            low = cand if low is None else jnp.minimum(low, cand)
        return jnp.logical_and(first == 0, jnp.min(low) >= SB_DEAD_LOG2 + SB_BOUND_MARGIN)

    def stage_c(item, slot, acc):
        qi, kb, first, live = item
        k_rows = pl.ds(pl.multiple_of(kb * SB_TK, SB_TK), SB_TK)
        new = jnp.where(first == 1, 0.0, acc)
        for h in range(SB_HPS):
            new = new + jnp.dot(w_ref[slot, h], vm_ref[h, k_rows, :], preferred_element_type=_F32)
        acc = jnp.where(live == 1, new, acc)
        o_ref[0, pl.ds(pl.multiple_of(qi * SB_TQ, SB_TQ), SB_TQ), :] = acc.astype(o_ref.dtype)
        return acc

    def key_blocks(qi):
        return (qi * SB_TQ + SB_TQ + KV_FRONT + SB_TK - 1) // SB_TK

    def advance(item, dead):
        qi, kb, _, live = item
        tile_done = jnp.logical_or(kb == 0, dead)
        nxt = jnp.where(tile_done, qi + 1, qi)
        live = jnp.where(nxt >= N_QT, 0, live)
        nxt = jnp.minimum(nxt, N_QT - 1)
        kb = jnp.where(tile_done, key_blocks(nxt) - 1, kb - 1)
        return nxt, kb, tile_done.astype(jnp.int32), live

    def step(items, slot, acc):
        cur, prev, prev2 = items
        acc = stage_c(prev2, slot, acc)
        stage_b(1 - slot)
        dead = jnp.logical_or(next_totals(cur, 1 - slot), next_block_dead(cur))
        stage_a(cur, slot)
        return (advance(cur, dead), cur, prev), acc

    def four_steps(carry):
        items, acc = carry
        for slot in (0, 1, 0, 1):
            items, acc = step(items, slot, acc)
        return items, acc

    def unfinished(carry):
        (cur, prev, prev2), _ = carry
        return cur[3] + prev[3] + prev2[3] > 0

    as_item = lambda *v: tuple(jnp.int32(x) for x in v)
    item0 = as_item(0, key_blocks(0) - 1, 1, 1)
    item1 = as_item(1, key_blocks(1) - 1, 1, 1)
    tot_ref[...] = jnp.zeros_like(tot_ref)
    stage_a(item0, 0)
    stage_b(0)
    dead = next_totals(item1, 0)
    stage_a(item1, 1)
    carry = ((advance(item1, dead), item1, item0), jnp.zeros((SB_TQ, SB_LW), _F32))
    lax.while_loop(unfinished, four_steps, carry)


def _stick_breaking(q, kv):
    j = jnp.arange(SB_TK)
    tri = (j[:, None] > j[None, :]).astype(_BF16)
    n_steps = SB_WIDTH // SB_LW
    slots = lambda dtype: pltpu.VMEM((2, SB_HPS, SB_TQ, SB_TK), dtype)
    return pl.pallas_call(
        _sb_kernel,
        grid=(BATCH, n_steps),
        in_specs=[pl.BlockSpec((1, LP, SB_LW), lambda b, p: (b, 0, p)),
                  pl.BlockSpec((1, LKV, SB_LW), lambda b, p: (b, 0, p)),
                  pl.BlockSpec((1, LKV, SB_LW), lambda b, p: (b, 0, p + n_steps)),
                  pl.BlockSpec((SB_TK, SB_TK), lambda b, p: (0, 0))],
        out_specs=pl.BlockSpec((1, LP, SB_LW), lambda b, p: (b, 0, p)),
        scratch_shapes=[pltpu.VMEM((SB_HPS, LP, SB_LW), _BF16),
                        pltpu.VMEM((SB_HPS, LKV, SB_LW), _BF16),
                        pltpu.VMEM((LKV // SB_TK, LANES, SB_LW), _BF16),
                        slots(_F32), slots(_BF16), slots(_BF16),
                        pltpu.VMEM((2, SB_HPS, SB_TQ, LANES), _F32),
                        pltpu.VMEM((SB_HPS, SB_TQ, LANES), _F32)],
        out_shape=jax.ShapeDtypeStruct((BATCH, LP, SB_WIDTH), _BF16),
        compiler_params=_params("arbitrary", "arbitrary"),
        name="stick_breaking",
    )(q, kv, kv, tri)


ROUTER_TM = 512


def _router_kernel(h_ref, w_ref, b_ref, tri_ref, cls_ref, rank_ref, cnt_ref, gext_ref, base_ref):
    @pl.when(pl.program_id(0) == 0)
    def _():
        base_ref[...] = jnp.zeros_like(base_ref)

    h = h_ref[...]
    h_hi = h.astype(_BF16)
    h_lo = (h - h_hi.astype(_F32)).astype(_BF16)
    w_hi, w_lo = w_ref[0], w_ref[1]
    nt = lambda a, b: lax.dot_general(a, b, (((1,), (1,)), ((), ())), preferred_element_type=_F32)
    logits = nt(w_hi, h_hi) + nt(w_hi, h_lo) + nt(w_lo, h_hi) + b_ref[...]
    m = jnp.max(logits, axis=0, keepdims=True)
    e = jnp.exp(logits - m)
    p = e / jnp.sum(e, axis=0, keepdims=True)
    ng = N_EXPERT_GROUPS
    pj = [p[j * ng:(j + 1) * ng, :] for j in range(EXPERTS_PER_GROUP)]
    m1 = jnp.maximum(jnp.maximum(pj[0], pj[1]), jnp.maximum(pj[2], pj[3]))
    i1 = jnp.where(pj[0] == m1, 0, jnp.where(pj[1] == m1, 1, jnp.where(pj[2] == m1, 2, 3)))
    qj = [jnp.where(i1 == j, -1.0, pj[j]) for j in range(EXPERTS_PER_GROUP)]
    m2 = jnp.maximum(jnp.maximum(qj[0], qj[1]), jnp.maximum(qj[2], qj[3]))
    i2 = jnp.where(qj[0] == m2, 0, jnp.where(qj[1] == m2, 1, jnp.where(qj[2] == m2, 2, 3)))
    score = m1 + m2
    gid = lax.broadcasted_iota(jnp.int32, score.shape, 0)
    best = jnp.max(score, axis=0, keepdims=True)
    gsel = jnp.min(jnp.where(score == best, gid, ng), axis=0, keepdims=True)
    chosen = gid == gsel
    pick_f = lambda v: jnp.sum(jnp.where(chosen, v, 0.0), axis=0, keepdims=True)
    pick_i = lambda v: jnp.sum(jnp.where(chosen, v, 0), axis=0, keepdims=True)
    v1, v2, j1, j2 = pick_f(m1), pick_f(m2), pick_i(i1), pick_i(i2)
    denom = v1 + v2
    g1 = v1 / denom
    g2 = v2 / denom
    first_low = j1 < j2
    lo = jnp.where(first_low, j1, j2)
    hi = jnp.where(first_low, j2, j1)
    base = jnp.where(lo == 0, 0, jnp.where(lo == 1, 3, 5))
    cls = gsel * N_PAIR_CLASSES + base + hi - lo - 1
    cls_ref[...] = cls

    member = lax.broadcasted_iota(jnp.int32, (CLS_ROWS, ROUTER_TM), 0) == cls
    onehot = jnp.where(member, 1.0, 0.0)
    upto = jnp.dot(onehot.astype(_BF16), tri_ref[...], preferred_element_type=_F32)
    before = base_ref[...]
    rank = jnp.sum(jnp.where(member, upto - 1.0 + before, 0.0), axis=0, keepdims=True)
    rank_ref[...] = rank.astype(jnp.int32)
    after = before + jnp.sum(onehot, axis=1, keepdims=True)
    base_ref[...] = after
    cnt_ref[...] = after[:, :LANES]

    gates = jnp.concatenate([jnp.where(first_low, g1, g2), jnp.where(first_low, g2, g1),
                             jnp.zeros((LANES - 2, ROUTER_TM), _F32)], axis=0)
    gext_ref[...] = gates.T


def _router(h, router_w, router_b):
    n = h.shape[0]
    perm = jnp.array([g * EXPERTS_PER_GROUP + j for j in range(EXPERTS_PER_GROUP)
                      for g in range(N_EXPERT_GROUPS)])
    w_t = router_w.astype(_F32).T[perm]
    w_hi = w_t.astype(_BF16)
    w_t = jnp.stack([w_hi, (w_t - w_hi.astype(_F32)).astype(_BF16)])
    b_t = router_b.astype(_F32)[perm].reshape(N_EXPERTS, 1)
    j = jnp.arange(ROUTER_TM)
    tri = (j[:, None] <= j[None, :]).astype(_BF16)
    out = pl.BlockSpec((1, ROUTER_TM), lambda i: (0, i))
    return pl.pallas_call(
        _router_kernel,
        grid=(n // ROUTER_TM,),
        in_specs=[pl.BlockSpec((ROUTER_TM, D_MODEL), lambda i: (i, 0)),
                  pl.BlockSpec((2, N_EXPERTS, D_MODEL), lambda i: (0, 0, 0)),
                  pl.BlockSpec((N_EXPERTS, 1), lambda i: (0, 0)),
                  pl.BlockSpec((ROUTER_TM, ROUTER_TM), lambda i: (0, 0))],
        out_specs=[out, out,
                   pl.BlockSpec((CLS_ROWS, LANES), lambda i: (0, 0)),
                   pl.BlockSpec((ROUTER_TM, LANES), lambda i: (i, 0))],
        out_shape=[jax.ShapeDtypeStruct((1, n), jnp.int32),
                   jax.ShapeDtypeStruct((1, n), jnp.int32),
                   jax.ShapeDtypeStruct((CLS_ROWS, LANES), _F32),
                   jax.ShapeDtypeStruct((n, LANES), _F32)],
        scratch_shapes=[pltpu.VMEM((CLS_ROWS, ROUTER_TM), _F32)],
        compiler_params=_params("arbitrary"),
        name="router",
    )(h, w_t, b_t, tri)


def _class_experts():
    lo, hi = [], []
    for g in range(N_EXPERT_GROUPS):
        for a in range(EXPERTS_PER_GROUP):
            for b in range(a + 1, EXPERTS_PER_GROUP):
                lo.append(g * EXPERTS_PER_GROUP + a)
                hi.append(g * EXPERTS_PER_GROUP + b)
    return jnp.array(lo, jnp.int32), jnp.array(hi, jnp.int32)


def _expert_kernel(elo_ref, ehi_ref, nused_ref, x_ref, wg1, wu1, wd1, wg2, wu2, wd2, o_ref):
    t = pl.program_id(0)

    @pl.when(t < nused_ref[0])
    def _():
        x = x_ref[:, :D_MODEL].astype(_BF16)
        gates = x_ref[:, D_MODEL:]

        def ffn(wg, wu, wd, gate):
            a = jnp.dot(x, wg[0], preferred_element_type=_F32)
            u = jnp.dot(x, wu[0], preferred_element_type=_F32)
            hid = (a * _sigmoid(a)) * u * gate
            return jnp.dot(hid.astype(_BF16), wd[0], preferred_element_type=_F32)

        o_ref[...] = ffn(wg1, wu1, wd1, gates[:, 0:1]) + ffn(wg2, wu2, wd2, gates[:, 1:2])

    @pl.when(t >= nused_ref[0])
    def _():
        o_ref[...] = jnp.zeros_like(o_ref)


def _experts(x_sorted, tile_lo, tile_hi, n_used, w_gate, w_up, w_down):
    up_spec = lambda which: pl.BlockSpec(
        (1, D_MODEL, D_EXPERT), lambda t, lo, hi, nu: ((lo, hi)[which][t], 0, 0))
    down_spec = lambda which: pl.BlockSpec(
        (1, D_EXPERT, D_MODEL), lambda t, lo, hi, nu: ((lo, hi)[which][t], 0, 0))
    row = lambda d: pl.BlockSpec((MOE_TM, d), lambda t, lo, hi, nu: (t, 0))
    grid_spec = pltpu.PrefetchScalarGridSpec(
        num_scalar_prefetch=3,
        grid=(MOE_TILES,),
        in_specs=[row(XROW), up_spec(0), up_spec(0), down_spec(0),
                  up_spec(1), up_spec(1), down_spec(1)],
        out_specs=row(D_MODEL),
    )
    return pl.pallas_call(
        _expert_kernel,
        grid_spec=grid_spec,
        out_shape=jax.ShapeDtypeStruct((MOE_ROWS, D_MODEL), _F32),
        compiler_params=_params("arbitrary"),
        name="experts",
    )(tile_lo, tile_hi, n_used, x_sorted, w_gate, w_up, w_down, w_gate, w_up, w_down)


SUB = 8


def _tiles(x):
    return x.reshape(x.shape[0] // SUB, SUB, x.shape[1])


def _issue_rows(row_copy, pos_ref, base, slot):
    def issue(g, carry):
        for u in range(SUB):
            p = pos_ref[base + g * SUB + u]
            row_copy(slot, g, u, lax.shift_right_logical(p, 3), p & (SUB - 1)).start(priority=u % 2)
        return carry
    lax.fori_loop(0, ROW_TM // SUB, issue, 0)


def _dispatch_kernel(pos_ref, h_ref, g_ref, init_ref, out_ref, stage_ref, sem):
    del init_ref
    s = pl.program_id(0)
    last = pl.num_programs(0) - 1
    slot = s % 2

    def row_copy(slot, g, u, tile, row):
        return pltpu.make_async_copy(stage_ref.at[slot, g, pl.ds(u, 1), :],
                                     out_ref.at[tile, pl.ds(row, 1), :], sem.at[slot])

    def wait_slot(slot):
        pltpu.make_async_copy(stage_ref.at[slot], out_ref.at[pl.ds(0, ROW_TM // SUB)],
                              sem.at[slot]).wait()

    @pl.when(s >= 2)
    def _():
        wait_slot(slot)

    stage_ref[slot, :, :, :D_MODEL] = h_ref[...]
    stage_ref[slot, :, :, D_MODEL:] = g_ref[...]
    _issue_rows(row_copy, pos_ref, s * ROW_TM, slot)

    @pl.when(s == last)
    def _():
        wait_slot(slot)
        wait_slot(1 - slot)


def _dispatch(pos, h, gext):
    n = h.shape[0]
    assert n // ROW_TM >= 2
    tile_rows = ROW_TM // SUB
    grid_spec = pltpu.PrefetchScalarGridSpec(
        num_scalar_prefetch=1,
        grid=(n // ROW_TM,),
        in_specs=[pl.BlockSpec((tile_rows, SUB, D_MODEL), lambda i, pos: (i, 0, 0)),
                  pl.BlockSpec((tile_rows, SUB, LANES), lambda i, pos: (i, 0, 0)),
                  pl.BlockSpec(memory_space=pl.ANY)],
        out_specs=pl.BlockSpec(memory_space=pl.ANY),
        scratch_shapes=[pltpu.VMEM((2, tile_rows, SUB, XROW), _F32), pltpu.SemaphoreType.DMA((2,))],
    )
    out = pl.pallas_call(
        _dispatch_kernel,
        grid_spec=grid_spec,
        out_shape=jax.ShapeDtypeStruct((MOE_ROWS // SUB, SUB, XROW), _F32),
        input_output_aliases={3: 0},
        compiler_params=_params("arbitrary"),
        name="dispatch",
    )(pos, _tiles(h), _tiles(gext), jnp.zeros((MOE_ROWS // SUB, SUB, XROW), _F32))
    return out.reshape(MOE_ROWS, XROW)


def _combine_kernel(pos_ref, h_ref, y_ref, g_ref, b_ref, o_ref, buf_ref, sem):
    s = pl.program_id(0)
    last = pl.num_programs(0) - 1
    slot = s % 2

    def row_copy(slot, g, u, tile, row):
        return pltpu.make_async_copy(y_ref.at[tile, pl.ds(row, 1), :],
                                     buf_ref.at[slot, g, pl.ds(u, 1), :], sem.at[slot])

    @pl.when(s == 0)
    def _():
        _issue_rows(row_copy, pos_ref, 0, 0)

    @pl.when(s < last)
    def _():
        _issue_rows(row_copy, pos_ref, (s + 1) * ROW_TM, 1 - slot)

    pltpu.make_async_copy(y_ref.at[pl.ds(0, ROW_TM // SUB)], buf_ref.at[slot], sem.at[slot]).wait()
    o_ref[...] = _layer_norm(ALPHA * h_ref[...] + buf_ref[slot], g_ref[...], b_ref[...])


def _combine(pos, h, y_sorted, g, b):
    n = h.shape[0]
    tile_rows = ROW_TM // SUB
    row = pl.BlockSpec((tile_rows, SUB, D_MODEL), lambda i, pos: (i, 0, 0))
    vec = pl.BlockSpec((1, D_MODEL), lambda i, pos: (0, 0))
    grid_spec = pltpu.PrefetchScalarGridSpec(
        num_scalar_prefetch=1,
        grid=(n // ROW_TM,),
        in_specs=[row, pl.BlockSpec(memory_space=pl.ANY), vec, vec],
        out_specs=row,
        scratch_shapes=[pltpu.VMEM((2, tile_rows, SUB, D_MODEL), _F32),
                        pltpu.SemaphoreType.DMA((2,))],
    )
    out = pl.pallas_call(
        _combine_kernel,
        grid_spec=grid_spec,
        out_shape=jax.ShapeDtypeStruct((n // SUB, SUB, D_MODEL), _F32),
        compiler_params=_params("arbitrary"),
        name="combine",
    )(pos, _tiles(h), _tiles(y_sorted), g.reshape(1, D_MODEL), b.reshape(1, D_MODEL))
    return out.reshape(n, D_MODEL)


def _moe_layer(h, router_w, router_b, w_gate, w_up, w_down, ln_g, ln_b):
    cls, rank, cnt, gext = _router(h, router_w, router_b)
    counts = cnt[:N_CLASSES, 0].astype(jnp.int32)
    tiles = (counts + MOE_TM - 1) // MOE_TM
    tile_end = jnp.cumsum(tiles)
    class_start = (tile_end - tiles) * MOE_TM
    classes = jnp.arange(N_CLASSES, dtype=jnp.int32)
    pos = jnp.sum(jnp.where(cls[0][:, None] == classes[None, :], class_start[None, :], 0),
                  axis=1) + rank[0]
    n_used = tile_end[-1]
    tile_ids = jnp.minimum(jnp.arange(MOE_TILES, dtype=jnp.int32), n_used - 1)
    tile_cls = jnp.sum((tile_ids[:, None] >= tile_end[None, :]).astype(jnp.int32), axis=1)
    class_lo, class_hi = _class_experts()
    x_sorted = _dispatch(pos, h, gext)
    y_sorted = _experts(x_sorted, class_lo[tile_cls], class_hi[tile_cls],
                        n_used.reshape(1).astype(jnp.int32), w_gate, w_up, w_down)
    return _combine(pos, h, y_sorted, ln_g, ln_b)


def kernel(x, meta_tokens, mamba_w_in, mamba_conv_w, mamba_conv_b, mamba_dt_bias, mamba_a_log,
           mamba_d_skip, mamba_norm_w, mamba_w_out, sb_w_q, sb_w_o, shared_w_k, shared_w_v,
           ln_mix_g, ln_mix_b, ln_ffn_g, ln_ffn_b, router_w, router_b, moe_w_gate, moe_w_up,
           moe_w_down):
    bsz = x.shape[0]
    meta = jnp.broadcast_to(meta_tokens.astype(x.dtype)[None], (bsz, N_META, D_MODEL))
    h = jnp.concatenate([jnp.zeros((bsz, PAD, D_MODEL), x.dtype), meta, x], axis=1)
    h = h.reshape(NP, D_MODEL)
    kv = None
    for layer in range(DEPTH):
        if layer < N_A_LAYERS:
            w_in = mamba_w_in[layer].astype(_BF16)
            w_dt = jnp.pad(w_in[:, ZX_DIM:], ((0, 0), (0, LANES - SSM_HEADS)))
            proj = _in_proj(h, w_in, mamba_conv_w[layer], mamba_conv_b[layer])
            y = _ssd(proj.reshape(BATCH, LP, ZX_DIM), h.reshape(BATCH, LP, D_MODEL), w_dt,
                     mamba_dt_bias[layer], mamba_a_log[layer], mamba_d_skip[layer])
            h = _mamba_out(y.reshape(NP, D_INNER), proj, mamba_norm_w[layer],
                           mamba_w_out[layer].astype(_BF16), h, ln_mix_g[layer], ln_mix_b[layer])
        else:
            j = layer - N_A_LAYERS
            if kv is None:
                w_kv = jnp.concatenate([shared_w_k, shared_w_v], axis=1).astype(_BF16)
                kv = _matmul(h, w_kv, 2 * SB_WIDTH, 512, 2 * SB_WIDTH, _BF16).reshape(BATCH, LP, 2 * SB_WIDTH)
                kv = jnp.pad(kv[:, PAD:], ((0, 0), (PAD + KV_FRONT, 0), (0, 0)))
            q = _matmul(h, sb_w_q[j].astype(_BF16), SB_WIDTH, 512, 1024, _BF16,
                        scale=SB_SCALE * LOG2_E)
            o = _stick_breaking(q.reshape(BATCH, LP, SB_WIDTH), kv)
            h = _attn_out(o.reshape(NP, SB_WIDTH), sb_w_o[j].astype(_BF16), h,
                          ln_mix_g[layer], ln_mix_b[layer])
        h = _moe_layer(h, router_w, router_b, moe_w_gate[layer].astype(_BF16),
                       moe_w_up[layer].astype(_BF16), moe_w_down[layer].astype(_BF16),
                       ln_ffn_g[layer], ln_ffn_b[layer])
    return h.reshape(bsz, LP, D_MODEL)[:, PAD + N_META:]
```

```python
import functools
import math

import jax
import jax.numpy as jnp
from jax import lax
from jax.experimental import pallas as pl
from jax.experimental.pallas import tpu as pltpu

D_MODEL = 1024
BATCH = 8
SEQ = 2048
DEPTH = 4
N_META = 16
N_A_LAYERS = DEPTH // 2
ALPHA = (2.0 * DEPTH) ** 0.25
LN_EPS = 1e-5
D_INNER = 2048
SSM_HEAD_DIM = 64
SSM_HEADS = 32
SSM_GROUPS = 4
D_STATE = 128
CONV_WIDTH = 4
CHUNK = 128
CONV_DIM = D_INNER + 2 * SSM_GROUPS * D_STATE
SB_HEADS = 16
SB_HEAD_DIM = 64
SB_WIDTH = SB_HEADS * SB_HEAD_DIM
SB_SCALE = SB_HEAD_DIM ** -0.5
LOG2_E = math.log2(math.e)
N_EXPERTS = 16
N_EXPERT_GROUPS = 4
EXPERTS_PER_GROUP = 4
D_EXPERT = 512

L_REAL = N_META + SEQ
PAD = (-L_REAL) % CHUNK
LP = L_REAL + PAD
N_CHUNKS = LP // CHUNK
NP = BATCH * LP
LANES = 128
N_PAIR_CLASSES = 6
N_CLASSES = N_EXPERT_GROUPS * N_PAIR_CLASSES
MOE_TM = 256
MOE_TILES = NP // MOE_TM + N_CLASSES
MOE_ROWS = MOE_TILES * MOE_TM
CLS_ROWS = 32
XROW = D_MODEL + LANES
ROW_TM = 512
VMEM_LIMIT = 48 * 1024 * 1024

_F32 = jnp.float32
_BF16 = jnp.bfloat16


def _params(*sem):
    return pltpu.CompilerParams(dimension_semantics=sem, vmem_limit_bytes=VMEM_LIMIT)


def _sigmoid(x):
    return 1.0 / (1.0 + jnp.exp(-x))


def _softplus(x):
    return jnp.maximum(x, 0.0) + jnp.log(1.0 + jnp.exp(-jnp.abs(x)))


def _layer_norm(t, g, b):
    mu = jnp.mean(t, axis=-1, keepdims=True)
    d = t - mu
    var = jnp.mean(d * d, axis=-1, keepdims=True)
    return d * lax.rsqrt(var + LN_EPS) * g + b


def _mm_kernel(x_ref, w_ref, o_ref, *, scale):
    x = x_ref[...].astype(_BF16)
    acc = jnp.dot(x, w_ref[...], preferred_element_type=_F32)
    if scale != 1.0:
        acc = acc * scale
    o_ref[...] = acc.astype(o_ref.dtype)


def _matmul(x, w, n_out, tm, tn, out_dtype, scale=1.0):
    m, k = x.shape
    return pl.pallas_call(
        functools.partial(_mm_kernel, scale=scale),
        grid=(n_out // tn, m // tm),
        in_specs=[pl.BlockSpec((tm, k), lambda j, i: (i, 0)),
                  pl.BlockSpec((k, tn), lambda j, i: (0, j))],
        out_specs=pl.BlockSpec((tm, tn), lambda j, i: (i, j)),
        out_shape=jax.ShapeDtypeStruct((m, n_out), out_dtype),
        compiler_params=_params("arbitrary", "arbitrary"),
        name="matmul",
    )(x, w)


def _add_ln_kernel(h_ref, m_ref, g_ref, b_ref, o_ref):
    t = ALPHA * h_ref[...] + m_ref[...]
    o_ref[...] = _layer_norm(t, g_ref[...], b_ref[...])


def _add_ln(h, mix, g, b, tm=512):
    n, d = h.shape
    row = pl.BlockSpec((tm, d), lambda i: (i, 0))
    vec = pl.BlockSpec((1, d), lambda i: (0, 0))
    return pl.pallas_call(
        _add_ln_kernel,
        grid=(n // tm,),
        in_specs=[row, row, vec, vec],
        out_specs=row,
        out_shape=jax.ShapeDtypeStruct((n, d), _F32),
        compiler_params=_params("arbitrary"),
        name="add_ln",
    )(h, mix, g.reshape(1, d), b.reshape(1, d))


ZX_DIM = D_INNER + CONV_DIM
PROJ_TM = LP // 4
PROJ_TN = ZX_DIM // 2
PROJ_CHUNK = 512
CONV_KEEP = 8


def _in_proj_kernel(x_ref, w_ref, cw_ref, cb_ref, o_ref, hist_ref):
    j = pl.program_id(0)
    seq_start = pl.program_id(1) % (LP // PROJ_TM) == 0
    x = x_ref[...].astype(_BF16)
    rows = lax.broadcasted_iota(jnp.int32, (PROJ_TM, PROJ_CHUNK), 0)
    real = jnp.logical_or(jnp.logical_not(seq_start), rows >= PAD)

    def plain(c):
        sl = slice(c * PROJ_CHUNK, (c + 1) * PROJ_CHUNK)
        o_ref[:, sl] = jnp.dot(x, w_ref[:, sl], preferred_element_type=_F32).astype(o_ref.dtype)

    def conv(c):
        sl = slice(c * PROJ_CHUNK, (c + 1) * PROJ_CHUNK)
        cur = jnp.where(real, jnp.dot(x, w_ref[:, sl], preferred_element_type=_F32), 0.0)
        prev = jnp.where(seq_start, 0.0, hist_ref[c])
        hist_ref[c] = cur[PROJ_TM - CONV_KEEP:, :]
        xx = jnp.concatenate([prev, cur], axis=0)
        w = cw_ref[:, sl]
        acc = cb_ref[:, sl] + w[CONV_WIDTH - 1:CONV_WIDTH, :] * cur
        for back in range(1, CONV_WIDTH):
            k = CONV_WIDTH - 1 - back
            acc = acc + w[k:k + 1, :] * pltpu.roll(xx, back, 0)[CONV_KEEP:, :]
        o_ref[:, sl] = (acc * _sigmoid(acc)).astype(o_ref.dtype)

    for block in range(ZX_DIM // PROJ_TN):
        @pl.when(j == block)
        def _(block=block):
            for c in range(PROJ_TN // PROJ_CHUNK):
                if block * PROJ_TN + c * PROJ_CHUNK < D_INNER:
                    plain(c)
                else:
                    conv(c)


def _in_proj(h, w_in, conv_w, conv_b):
    n, k = h.shape
    cw = jnp.pad(conv_w.astype(_F32), ((0, 0), (D_INNER, 0)))
    cb = jnp.pad(conv_b.astype(_F32), (D_INNER, 0)).reshape(1, ZX_DIM)
    return pl.pallas_call(
        _in_proj_kernel,
        grid=(ZX_DIM // PROJ_TN, n // PROJ_TM),
        in_specs=[pl.BlockSpec((PROJ_TM, k), lambda j, i: (i, 0)),
                  pl.BlockSpec((k, PROJ_TN), lambda j, i: (0, j)),
                  pl.BlockSpec((CONV_WIDTH, PROJ_TN), lambda j, i: (0, j)),
                  pl.BlockSpec((1, PROJ_TN), lambda j, i: (0, j))],
        out_specs=pl.BlockSpec((PROJ_TM, PROJ_TN), lambda j, i: (i, j)),
        out_shape=jax.ShapeDtypeStruct((n, ZX_DIM), _BF16),
        scratch_shapes=[pltpu.VMEM((PROJ_TN // PROJ_CHUNK, CONV_KEEP, PROJ_CHUNK), _F32)],
        compiler_params=_params("arbitrary", "arbitrary"),
        name="in_proj",
    )(h, w_in, cw, cb)


N_HEAD_PAIRS = SSM_HEADS // 2
PAIRS_PER_GROUP = N_HEAD_PAIRS // SSM_GROUPS


def _ssd_kernel(xs_ref, b_ref, c_ref, h_ref, wdt_ref, dtb_ref, aneg_ref, dskip_ref, ltri_ref,
                exp_ref, y_ref, state_ref):
    c = pl.program_id(1)

    @pl.when(c == 0)
    def _():
        state_ref[...] = jnp.zeros_like(state_ref)

    rows = lax.broadcasted_iota(jnp.int32, (CHUNK, LANES), 0)
    cols = lax.broadcasted_iota(jnp.int32, (CHUNK, LANES), 1)
    causal = cols <= rows
    left = cols < SSM_HEAD_DIM

    dt_raw = jnp.dot(h_ref[0].astype(_BF16), wdt_ref[...], preferred_element_type=_F32)
    dt = _softplus(dt_raw + dtb_ref[...])
    dt = jnp.where(rows + c * CHUNK >= PAD, dt, 0.0)
    a = dt * aneg_ref[...]
    a_cum = jnp.dot(ltri_ref[...], a, preferred_element_type=_F32,
                    precision=lax.Precision.HIGHEST)
    a_cum_t = a_cum.T
    total = a_cum[CHUNK - 1:CHUNK, :]
    to_end = jnp.exp(total - a_cum)
    chunk_decay = jnp.exp(jnp.broadcast_to(total, (8, LANES)))

    expand = exp_ref[...]
    dt_x = jnp.dot(dt.astype(_BF16), expand, preferred_element_type=_F32)
    to_end_x = jnp.dot(to_end.astype(_BF16), expand, preferred_element_type=_F32)
    chunk_decay_x = jnp.dot(chunk_decay.astype(_BF16), expand, preferred_element_type=_F32)

    for g in range(SSM_GROUPS):
        bg = b_ref[0, :, g * D_STATE:(g + 1) * D_STATE]
        cg = c_ref[0, :, g * D_STATE:(g + 1) * D_STATE]
        cb = lax.dot_general(cg, bg, (((1,), (1,)), ((), ())), preferred_element_type=_F32)
        bg_t = bg.astype(_F32).T.astype(_BF16)
        cg32 = cg.astype(_F32)
        for jj in range(PAIRS_PER_GROUP):
            j = g * PAIRS_PER_GROUP + jj
            sl = slice(j * LANES, (j + 1) * LANES)
            x2 = xs_ref[0, :, sl].astype(_F32)
            xdt = x2 * dt_x[:, sl]
            parts, from_start = [], []
            for h in (2 * j, 2 * j + 1):
                a_col = jnp.broadcast_to(a_cum[:, h:h + 1], (CHUNK, CHUNK))
                decay = jnp.where(causal, jnp.exp(a_col - a_cum_t[h:h + 1, :]), 0.0)
                parts.append((cb * decay).astype(_BF16))
                from_start.append(jnp.exp(a_col))
            for fs in from_start:
                parts.append((cg32 * fs).astype(_BF16))
            lhs = jnp.concatenate(parts, axis=1)
            state = state_ref[j]
            xb = xdt.astype(_BF16)
            sb = state.astype(_BF16)
            zero = jnp.zeros_like(xb)
            rhs = jnp.concatenate([jnp.where(left, xb, zero), jnp.where(left, zero, xb),
                                   jnp.where(left, sb, zero), jnp.where(left, zero, sb)], axis=0)
            y2 = jnp.dot(lhs, rhs, preferred_element_type=_F32)
            y2 = y2 + dskip_ref[:, sl] * x2
            y_ref[0, :, sl] = y2.astype(y_ref.dtype)
            upd = jnp.dot(bg_t, (xdt * to_end_x[:, sl]).astype(_BF16), preferred_element_type=_F32)
            state_ref[j] = state * chunk_decay_x[0:1, sl] + upd


def _ssd(proj, h, w_dt, dt_bias, a_log, d_skip):
    pad_h = LANES - SSM_HEADS
    dtb = jnp.pad(dt_bias.astype(_F32), (0, pad_h)).reshape(1, LANES)
    aneg = jnp.pad(-jnp.exp(a_log.astype(_F32)), (0, pad_h)).reshape(1, LANES)
    dskip = jnp.repeat(d_skip.astype(_F32), SSM_HEAD_DIM).reshape(1, D_INNER)
    ltri = jnp.tril(jnp.ones((CHUNK, CHUNK), _F32))
    expand = (jnp.arange(LANES)[:, None] == (jnp.arange(D_INNER)[None, :] // SSM_HEAD_DIM)).astype(_BF16)
    gn = SSM_GROUPS * D_STATE
    vec = lambda n: pl.BlockSpec((1, n), lambda b, c: (0, 0))
    return pl.pallas_call(
        _ssd_kernel,
        grid=(BATCH, N_CHUNKS),
        in_specs=[pl.BlockSpec((1, CHUNK, D_INNER), lambda b, c: (b, c, 1)),
                  pl.BlockSpec((1, CHUNK, gn), lambda b, c: (b, c, 2 * D_INNER // gn)),
                  pl.BlockSpec((1, CHUNK, gn), lambda b, c: (b, c, 2 * D_INNER // gn + 1)),
                  pl.BlockSpec((1, CHUNK, D_MODEL), lambda b, c: (b, c, 0)),
                  pl.BlockSpec((D_MODEL, LANES), lambda b, c: (0, 0)),
                  vec(LANES), vec(LANES), vec(D_INNER),
                  pl.BlockSpec((CHUNK, CHUNK), lambda b, c: (0, 0)),
                  pl.BlockSpec((LANES, D_INNER), lambda b, c: (0, 0))],
        out_specs=pl.BlockSpec((1, CHUNK, D_INNER), lambda b, c: (b, c, 0)),
        out_shape=jax.ShapeDtypeStruct((BATCH, LP, D_INNER), _BF16),
        scratch_shapes=[pltpu.VMEM((N_HEAD_PAIRS, D_STATE, LANES), _F32)],
        compiler_params=_params("arbitrary", "arbitrary"),
        name="ssd",
    )(proj, proj, proj, h, w_dt, dtb, aneg, dskip, ltri, expand)


def _mamba_out_kernel(y_ref, z_ref, nw_ref, w_ref, h_ref, g_ref, b_ref, o_ref):
    y = y_ref[...].astype(_F32)
    z = z_ref[...].astype(_F32)
    yg = y * (z * _sigmoid(z))
    gw = D_INNER // SSM_GROUPS
    parts = []
    for g in range(SSM_GROUPS):
        blk = yg[:, g * gw:(g + 1) * gw]
        ms = jnp.mean(blk * blk, axis=-1, keepdims=True)
        parts.append(blk * lax.rsqrt(ms + LN_EPS))
    yn = jnp.concatenate(parts, axis=1) * nw_ref[...]
    mix = jnp.dot(yn.astype(_BF16), w_ref[...], preferred_element_type=_F32)
    o_ref[...] = _layer_norm(ALPHA * h_ref[...] + mix, g_ref[...], b_ref[...])


def _mamba_out(y, proj, norm_w, w_out, h, g, b, tm=256):
    n = h.shape[0]
    vec = lambda d: pl.BlockSpec((1, d), lambda i: (0, 0))
    return pl.pallas_call(
        _mamba_out_kernel,
        grid=(n // tm,),
        in_specs=[pl.BlockSpec((tm, D_INNER), lambda i: (i, 0)),
                  pl.BlockSpec((tm, D_INNER), lambda i: (i, 0)),
                  vec(D_INNER),
                  pl.BlockSpec((D_INNER, D_MODEL), lambda i: (0, 0)),
                  pl.BlockSpec((tm, D_MODEL), lambda i: (i, 0)),
                  vec(D_MODEL), vec(D_MODEL)],
        out_specs=pl.BlockSpec((tm, D_MODEL), lambda i: (i, 0)),
        out_shape=jax.ShapeDtypeStruct((n, D_MODEL), _F32),
        compiler_params=_params("arbitrary"),
        name="mamba_out",
    )(y, proj, norm_w.reshape(1, D_INNER).astype(_F32), w_out, h,
      g.reshape(1, D_MODEL), b.reshape(1, D_MODEL))


def _attn_out_kernel(o_ref_in, w_ref, h_ref, g_ref, b_ref, o_ref):
    mix = jnp.dot(o_ref_in[...], w_ref[...], preferred_element_type=_F32)
    o_ref[...] = _layer_norm(ALPHA * h_ref[...] + mix, g_ref[...], b_ref[...])


def _attn_out(o, w_o, h, g, b, tm=512):
    n = h.shape[0]
    vec = pl.BlockSpec((1, D_MODEL), lambda i: (0, 0))
    return pl.pallas_call(
        _attn_out_kernel,
        grid=(n // tm,),
        in_specs=[pl.BlockSpec((tm, SB_WIDTH), lambda i: (i, 0)),
                  pl.BlockSpec((SB_WIDTH, D_MODEL), lambda i: (0, 0)),
                  pl.BlockSpec((tm, D_MODEL), lambda i: (i, 0)),
                  vec, vec],
        out_specs=pl.BlockSpec((tm, D_MODEL), lambda i: (i, 0)),
        out_shape=jax.ShapeDtypeStruct((n, D_MODEL), _F32),
        compiler_params=_params("arbitrary"),
        name="attn_out",
    )(o, w_o, h, g.reshape(1, D_MODEL), b.reshape(1, D_MODEL))


SB_TQ = 128
SB_TK = 256
KV_FRONT = SB_TK - SB_TQ
LKV = LP + KV_FRONT


SB_HPS = 4
SB_LW = SB_HPS * SB_HEAD_DIM
N_QT = LP // SB_TQ
MASKED_SCORE = -1e30
SB_DEAD_LOG2 = 150.0
SB_BOUND_MARGIN = 8.0


def _sb_kernel(q_ref, k_ref, v_ref, tri_ref, o_ref,
               qm_ref, km_ref, vm_ref, kbm_ref, lb_ref, sp_ref, w_ref, rs_ref, tot_ref):
    tri = tri_ref[...]
    col_minus_row = (lax.broadcasted_iota(jnp.int32, (SB_TQ, SB_TK), 1)
                     - lax.broadcasted_iota(jnp.int32, (SB_TQ, SB_TK), 0))

    def mask_q(c, carry):
        rows = pl.ds(pl.multiple_of(c * SB_TQ, SB_TQ), SB_TQ)
        blk = q_ref[0, rows, :]
        head = lax.broadcasted_iota(jnp.int32, blk.shape, 1) // SB_HEAD_DIM
        for h in range(SB_HPS):
            qm_ref[h, rows, :] = jnp.where(head == h, blk, jnp.zeros_like(blk))
        return carry

    def fill_block(c, kblk, vblk):
        rows = pl.ds(pl.multiple_of(c * SB_TK, SB_TK), SB_TK)
        km_ref[rows, :] = kblk
        head = lax.broadcasted_iota(jnp.int32, vblk.shape, 1) // SB_HEAD_DIM
        for h in range(SB_HPS):
            vm_ref[h, rows, :] = jnp.where(head == h, vblk, jnp.zeros_like(vblk))
        kbar = jnp.mean(kblk.astype(_F32), axis=0, keepdims=True)
        mean_rows = jnp.broadcast_to(kbar, (LANES, SB_LW))
        row_id = lax.broadcasted_iota(jnp.int32, (LANES, SB_LW), 0)
        row_head = lax.broadcasted_iota(jnp.int32, (LANES, SB_LW), 1) // SB_HEAD_DIM
        kbm_ref[c] = jnp.where(row_id == row_head, mean_rows, 0.0).astype(_BF16)

    def first_block(ref):
        n = SB_TK - KV_FRONT
        blk = ref[0, 0:n, :]
        real = lax.broadcasted_iota(jnp.int32, blk.shape, 0) >= PAD
        return jnp.concatenate([jnp.zeros((KV_FRONT, SB_LW), blk.dtype),
                                jnp.where(real, blk, jnp.zeros_like(blk))], axis=0)

    def fill(c, carry):
        src = pl.ds(pl.multiple_of(c * SB_TK - KV_FRONT, KV_FRONT), SB_TK)
        fill_block(c, k_ref[0, src, :], v_ref[0, src, :])
        return carry

    lax.fori_loop(0, N_QT, mask_q, 0)
    fill_block(0, first_block(k_ref), first_block(v_ref))
    lax.fori_loop(1, LKV // SB_TK, fill, 0)

    def stage_a(item, slot):
        qi, kb = item[0], item[1]
        q_rows = pl.ds(pl.multiple_of(qi * SB_TQ, SB_TQ), SB_TQ)
        kblk = km_ref[pl.ds(pl.multiple_of(kb * SB_TK, SB_TK), SB_TK), :]
        valid = col_minus_row < qi * SB_TQ + KV_FRONT - kb * SB_TK
        for h in range(SB_HPS):
            z = lax.dot_general(qm_ref[h, q_rows, :], kblk, (((1,), (1,)), ((), ())),
                                preferred_element_type=_F32)
            z = jnp.where(valid, z, MASKED_SCORE)
            neg_abs = pltpu.bitcast(pltpu.bitcast(z, jnp.uint32) | jnp.uint32(0x80000000), _F32)
            s = jnp.maximum(z, 0.0) + jnp.log2(1.0 + jnp.exp2(neg_abs))
            lb_ref[slot, h] = z - s
            sp_ref[slot, h] = s.astype(_BF16)
            rs_ref[slot, h] = jnp.broadcast_to(jnp.sum(s, axis=-1, keepdims=True), (SB_TQ, LANES))

    def stage_b(slot):
        for h in range(SB_HPS):
            later = jnp.dot(sp_ref[slot, h], tri, preferred_element_type=_F32)
            total = jnp.tile(tot_ref[h], (1, SB_TK // LANES))
            w_ref[slot, h] = jnp.exp2(lb_ref[slot, h] - later - total).astype(_BF16)

    def next_totals(item, prev_slot):
        fresh = item[2] == 1
        low = None
        for h in range(SB_HPS):
            tot = jnp.where(fresh, 0.0, tot_ref[h] + rs_ref[prev_slot, h])
            tot_ref[h] = tot
            low = tot if low is None else jnp.minimum(low, tot)
        return jnp.min(low) >= SB_DEAD_LOG2

    def next_block_dead(item):
        qi, kb, first, _ = item
        q = q_ref[0, pl.ds(pl.multiple_of(qi * SB_TQ, SB_TQ), SB_TQ), :]
        zbar = lax.dot_general(q, kbm_ref[kb], (((1,), (1,)), ((), ())),
                               preferred_element_type=_F32)
        zbar = zbar - 0.02 * jnp.abs(zbar) - 0.05
        bound = SB_TK * (jnp.maximum(zbar, 0.0) + jnp.log2(1.0 + jnp.exp2(-jnp.abs(zbar))))
        lane = lax.broadcasted_iota(jnp.int32, bound.shape, 1)
        low = None
        for h in range(SB_HPS):
            cand = tot_ref[h] + jnp.where(lane == h, bound, -MASKED_SCORE)
            low = cand if low is None else jnp.minimum(low, cand)
        return jnp.logical_and(first == 0, jnp.min(low) >= SB_DEAD_LOG2 + SB_BOUND_MARGIN)

    def stage_c(item, slot, acc):
        qi, kb, first, live = item
        k_rows = pl.ds(pl.multiple_of(kb * SB_TK, SB_TK), SB_TK)
        new = jnp.where(first == 1, 0.0, acc)
        for h in range(SB_HPS):
            new = new + jnp.dot(w_ref[slot, h], vm_ref[h, k_rows, :], preferred_element_type=_F32)
        acc = jnp.where(live == 1, new, acc)
        o_ref[0, pl.ds(pl.multiple_of(qi * SB_TQ, SB_TQ), SB_TQ), :] = acc.astype(o_ref.dtype)
        return acc

    def key_blocks(qi):
        return (qi * SB_TQ + SB_TQ + KV_FRONT + SB_TK - 1) // SB_TK

    def advance(item, dead):
        qi, kb, _, live = item
        tile_done = jnp.logical_or(kb == 0, dead)
        nxt = jnp.where(tile_done, qi + 1, qi)
        live = jnp.where(nxt >= N_QT, 0, live)
        nxt = jnp.minimum(nxt, N_QT - 1)
        kb = jnp.where(tile_done, key_blocks(nxt) - 1, kb - 1)
        return nxt, kb, tile_done.astype(jnp.int32), live

    def step(items, slot, acc):
        cur, prev, prev2 = items
        acc = stage_c(prev2, slot, acc)
        stage_b(1 - slot)
        dead = jnp.logical_or(next_totals(cur, 1 - slot), next_block_dead(cur))
        stage_a(cur, slot)
        return (advance(cur, dead), cur, prev), acc

    def four_steps(carry):
        items, acc = carry
        for slot in (0, 1, 0, 1):
            items, acc = step(items, slot, acc)
        return items, acc

    def unfinished(carry):
        (cur, prev, prev2), _ = carry
        return cur[3] + prev[3] + prev2[3] > 0

    as_item = lambda *v: tuple(jnp.int32(x) for x in v)
    item0 = as_item(0, key_blocks(0) - 1, 1, 1)
    item1 = as_item(1, key_blocks(1) - 1, 1, 1)
    tot_ref[...] = jnp.zeros_like(tot_ref)
    stage_a(item0, 0)
    stage_b(0)
    dead = next_totals(item1, 0)
    stage_a(item1, 1)
    carry = ((advance(item1, dead), item1, item0), jnp.zeros((SB_TQ, SB_LW), _F32))
    lax.while_loop(unfinished, four_steps, carry)


def _stick_breaking(q, kv):
    j = jnp.arange(SB_TK)
    tri = (j[:, None] > j[None, :]).astype(_BF16)
    n_steps = SB_WIDTH // SB_LW
    slots = lambda dtype: pltpu.VMEM((2, SB_HPS, SB_TQ, SB_TK), dtype)
    return pl.pallas_call(
        _sb_kernel,
        grid=(BATCH, n_steps),
        in_specs=[pl.BlockSpec((1, LP, SB_LW), lambda b, p: (b, 0, p)),
                  pl.BlockSpec((1, LP, SB_LW), lambda b, p: (b, 0, p)),
                  pl.BlockSpec((1, LP, SB_LW), lambda b, p: (b, 0, p + n_steps)),
                  pl.BlockSpec((SB_TK, SB_TK), lambda b, p: (0, 0))],
        out_specs=pl.BlockSpec((1, LP, SB_LW), lambda b, p: (b, 0, p)),
        scratch_shapes=[pltpu.VMEM((SB_HPS, LP, SB_LW), _BF16),
                        pltpu.VMEM((LKV, SB_LW), _BF16),
                        pltpu.VMEM((SB_HPS, LKV, SB_LW), _BF16),
                        pltpu.VMEM((LKV // SB_TK, LANES, SB_LW), _BF16),
                        slots(_F32), slots(_BF16), slots(_BF16),
                        pltpu.VMEM((2, SB_HPS, SB_TQ, LANES), _F32),
                        pltpu.VMEM((SB_HPS, SB_TQ, LANES), _F32)],
        out_shape=jax.ShapeDtypeStruct((BATCH, LP, SB_WIDTH), _BF16),
        compiler_params=_params("arbitrary", "arbitrary"),
        name="stick_breaking",
    )(q, kv, kv, tri)


ROUTER_TM = 512


def _router_kernel(h_ref, w_ref, b_ref, tri_ref, cls_ref, rank_ref, cnt_ref, gext_ref, base_ref):
    @pl.when(pl.program_id(0) == 0)
    def _():
        base_ref[...] = jnp.zeros_like(base_ref)

    h = h_ref[...]
    h_hi = h.astype(_BF16)
    h_lo = (h - h_hi.astype(_F32)).astype(_BF16)
    w_hi, w_lo = w_ref[0], w_ref[1]
    nt = lambda a, b: lax.dot_general(a, b, (((1,), (1,)), ((), ())), preferred_element_type=_F32)
    logits = nt(w_hi, h_hi) + nt(w_hi, h_lo) + nt(w_lo, h_hi) + b_ref[...]
    m = jnp.max(logits, axis=0, keepdims=True)
    e = jnp.exp(logits - m)
    p = e / jnp.sum(e, axis=0, keepdims=True)
    ng = N_EXPERT_GROUPS
    pj = [p[j * ng:(j + 1) * ng, :] for j in range(EXPERTS_PER_GROUP)]
    m1 = jnp.maximum(jnp.maximum(pj[0], pj[1]), jnp.maximum(pj[2], pj[3]))
    i1 = jnp.where(pj[0] == m1, 0, jnp.where(pj[1] == m1, 1, jnp.where(pj[2] == m1, 2, 3)))
    qj = [jnp.where(i1 == j, -1.0, pj[j]) for j in range(EXPERTS_PER_GROUP)]
    m2 = jnp.maximum(jnp.maximum(qj[0], qj[1]), jnp.maximum(qj[2], qj[3]))
    i2 = jnp.where(qj[0] == m2, 0, jnp.where(qj[1] == m2, 1, jnp.where(qj[2] == m2, 2, 3)))
    score = m1 + m2
    gid = lax.broadcasted_iota(jnp.int32, score.shape, 0)
    best = jnp.max(score, axis=0, keepdims=True)
    gsel = jnp.min(jnp.where(score == best, gid, ng), axis=0, keepdims=True)
    chosen = gid == gsel
    pick_f = lambda v: jnp.sum(jnp.where(chosen, v, 0.0), axis=0, keepdims=True)
    pick_i = lambda v: jnp.sum(jnp.where(chosen, v, 0), axis=0, keepdims=True)
    v1, v2, j1, j2 = pick_f(m1), pick_f(m2), pick_i(i1), pick_i(i2)
    denom = v1 + v2
    g1 = v1 / denom
    g2 = v2 / denom
    first_low = j1 < j2
    lo = jnp.where(first_low, j1, j2)
    hi = jnp.where(first_low, j2, j1)
    base = jnp.where(lo == 0, 0, jnp.where(lo == 1, 3, 5))
    cls = gsel * N_PAIR_CLASSES + base + hi - lo - 1
    cls_ref[...] = cls

    member = lax.broadcasted_iota(jnp.int32, (CLS_ROWS, ROUTER_TM), 0) == cls
    onehot = jnp.where(member, 1.0, 0.0)
    upto = jnp.dot(onehot.astype(_BF16), tri_ref[...], preferred_element_type=_F32)
    before = base_ref[...]
    rank = jnp.sum(jnp.where(member, upto - 1.0 + before, 0.0), axis=0, keepdims=True)
    rank_ref[...] = rank.astype(jnp.int32)
    after = before + jnp.sum(onehot, axis=1, keepdims=True)
    base_ref[...] = after
    cnt_ref[...] = after[:, :LANES]

    gates = jnp.concatenate([jnp.where(first_low, g1, g2), jnp.where(first_low, g2, g1),
                             jnp.zeros((LANES - 2, ROUTER_TM), _F32)], axis=0)
    gext_ref[...] = gates.T


def _router(h, router_w, router_b):
    n = h.shape[0]
    perm = jnp.array([g * EXPERTS_PER_GROUP + j for j in range(EXPERTS_PER_GROUP)
                      for g in range(N_EXPERT_GROUPS)])
    w_t = router_w.astype(_F32).T[perm]
    w_hi = w_t.astype(_BF16)
    w_t = jnp.stack([w_hi, (w_t - w_hi.astype(_F32)).astype(_BF16)])
    b_t = router_b.astype(_F32)[perm].reshape(N_EXPERTS, 1)
    j = jnp.arange(ROUTER_TM)
    tri = (j[:, None] <= j[None, :]).astype(_BF16)
    out = pl.BlockSpec((1, ROUTER_TM), lambda i: (0, i))
    return pl.pallas_call(
        _router_kernel,
        grid=(n // ROUTER_TM,),
        in_specs=[pl.BlockSpec((ROUTER_TM, D_MODEL), lambda i: (i, 0)),
                  pl.BlockSpec((2, N_EXPERTS, D_MODEL), lambda i: (0, 0, 0)),
                  pl.BlockSpec((N_EXPERTS, 1), lambda i: (0, 0)),
                  pl.BlockSpec((ROUTER_TM, ROUTER_TM), lambda i: (0, 0))],
        out_specs=[out, out,
                   pl.BlockSpec((CLS_ROWS, LANES), lambda i: (0, 0)),
                   pl.BlockSpec((ROUTER_TM, LANES), lambda i: (i, 0))],
        out_shape=[jax.ShapeDtypeStruct((1, n), jnp.int32),
                   jax.ShapeDtypeStruct((1, n), jnp.int32),
                   jax.ShapeDtypeStruct((CLS_ROWS, LANES), _F32),
                   jax.ShapeDtypeStruct((n, LANES), _F32)],
        scratch_shapes=[pltpu.VMEM((CLS_ROWS, ROUTER_TM), _F32)],
        compiler_params=_params("arbitrary"),
        name="router",
    )(h, w_t, b_t, tri)


def _class_experts():
    lo, hi = [], []
    for g in range(N_EXPERT_GROUPS):
        for a in range(EXPERTS_PER_GROUP):
            for b in range(a + 1, EXPERTS_PER_GROUP):
                lo.append(g * EXPERTS_PER_GROUP + a)
                hi.append(g * EXPERTS_PER_GROUP + b)
    return jnp.array(lo, jnp.int32), jnp.array(hi, jnp.int32)


def _expert_kernel(elo_ref, ehi_ref, nused_ref, x_ref, wg1, wu1, wd1, wg2, wu2, wd2, o_ref):
    t = pl.program_id(0)

    @pl.when(t < nused_ref[0])
    def _():
        x = x_ref[:, :D_MODEL].astype(_BF16)
        gates = x_ref[:, D_MODEL:]

        def ffn(wg, wu, wd, gate):
            a = jnp.dot(x, wg[0], preferred_element_type=_F32)
            u = jnp.dot(x, wu[0], preferred_element_type=_F32)
            hid = (a * _sigmoid(a)) * u * gate
            return jnp.dot(hid.astype(_BF16), wd[0], preferred_element_type=_F32)

        o_ref[...] = ffn(wg1, wu1, wd1, gates[:, 0:1]) + ffn(wg2, wu2, wd2, gates[:, 1:2])

    @pl.when(t >= nused_ref[0])
    def _():
        o_ref[...] = jnp.zeros_like(o_ref)


def _experts(x_sorted, tile_lo, tile_hi, n_used, w_gate, w_up, w_down):
    up_spec = lambda which: pl.BlockSpec(
        (1, D_MODEL, D_EXPERT), lambda t, lo, hi, nu: ((lo, hi)[which][t], 0, 0))
    down_spec = lambda which: pl.BlockSpec(
        (1, D_EXPERT, D_MODEL), lambda t, lo, hi, nu: ((lo, hi)[which][t], 0, 0))
    row = lambda d: pl.BlockSpec((MOE_TM, d), lambda t, lo, hi, nu: (t, 0))
    grid_spec = pltpu.PrefetchScalarGridSpec(
        num_scalar_prefetch=3,
        grid=(MOE_TILES,),
        in_specs=[row(XROW), up_spec(0), up_spec(0), down_spec(0),
                  up_spec(1), up_spec(1), down_spec(1)],
        out_specs=row(D_MODEL),
    )
    return pl.pallas_call(
        _expert_kernel,
        grid_spec=grid_spec,
        out_shape=jax.ShapeDtypeStruct((MOE_ROWS, D_MODEL), _F32),
        compiler_params=_params("arbitrary"),
        name="experts",
    )(tile_lo, tile_hi, n_used, x_sorted, w_gate, w_up, w_down, w_gate, w_up, w_down)


SUB = 8


def _tiles(x):
    return x.reshape(x.shape[0] // SUB, SUB, x.shape[1])


def _issue_rows(row_copy, pos_ref, base, slot):
    def issue(g, carry):
        for u in range(SUB):
            p = pos_ref[base + g * SUB + u]
            row_copy(slot, g, u, lax.shift_right_logical(p, 3), p & (SUB - 1)).start(priority=u % 2)
        return carry
    lax.fori_loop(0, ROW_TM // SUB, issue, 0)


def _dispatch_kernel(pos_ref, h_ref, g_ref, init_ref, out_ref, stage_ref, sem):
    del init_ref
    s = pl.program_id(0)
    last = pl.num_programs(0) - 1
    slot = s % 2

    def row_copy(slot, g, u, tile, row):
        return pltpu.make_async_copy(stage_ref.at[slot, g, pl.ds(u, 1), :],
                                     out_ref.at[tile, pl.ds(row, 1), :], sem.at[slot])

    def wait_slot(slot):
        pltpu.make_async_copy(stage_ref.at[slot], out_ref.at[pl.ds(0, ROW_TM // SUB)],
                              sem.at[slot]).wait()

    @pl.when(s >= 2)
    def _():
        wait_slot(slot)

    stage_ref[slot, :, :, :D_MODEL] = h_ref[...]
    stage_ref[slot, :, :, D_MODEL:] = g_ref[...]
    _issue_rows(row_copy, pos_ref, s * ROW_TM, slot)

    @pl.when(s == last)
    def _():
        wait_slot(slot)
        wait_slot(1 - slot)


def _dispatch(pos, h, gext):
    n = h.shape[0]
    assert n // ROW_TM >= 2
    tile_rows = ROW_TM // SUB
    grid_spec = pltpu.PrefetchScalarGridSpec(
        num_scalar_prefetch=1,
        grid=(n // ROW_TM,),
        in_specs=[pl.BlockSpec((tile_rows, SUB, D_MODEL), lambda i, pos: (i, 0, 0)),
                  pl.BlockSpec((tile_rows, SUB, LANES), lambda i, pos: (i, 0, 0)),
                  pl.BlockSpec(memory_space=pl.ANY)],
        out_specs=pl.BlockSpec(memory_space=pl.ANY),
        scratch_shapes=[pltpu.VMEM((2, tile_rows, SUB, XROW), _F32), pltpu.SemaphoreType.DMA((2,))],
    )
    out = pl.pallas_call(
        _dispatch_kernel,
        grid_spec=grid_spec,
        out_shape=jax.ShapeDtypeStruct((MOE_ROWS // SUB, SUB, XROW), _F32),
        input_output_aliases={3: 0},
        compiler_params=_params("arbitrary"),
        name="dispatch",
    )(pos, _tiles(h), _tiles(gext), jnp.zeros((MOE_ROWS // SUB, SUB, XROW), _F32))
    return out.reshape(MOE_ROWS, XROW)


def _combine_kernel(pos_ref, h_ref, y_ref, g_ref, b_ref, o_ref, buf_ref, sem):
    s = pl.program_id(0)
    last = pl.num_programs(0) - 1
    slot = s % 2

    def row_copy(slot, g, u, tile, row):
        return pltpu.make_async_copy(y_ref.at[tile, pl.ds(row, 1), :],
                                     buf_ref.at[slot, g, pl.ds(u, 1), :], sem.at[slot])

    @pl.when(s == 0)
    def _():
        _issue_rows(row_copy, pos_ref, 0, 0)

    @pl.when(s < last)
    def _():
        _issue_rows(row_copy, pos_ref, (s + 1) * ROW_TM, 1 - slot)

    pltpu.make_async_copy(y_ref.at[pl.ds(0, ROW_TM // SUB)], buf_ref.at[slot], sem.at[slot]).wait()
    o_ref[...] = _layer_norm(ALPHA * h_ref[...] + buf_ref[slot], g_ref[...], b_ref[...])


def _combine(pos, h, y_sorted, g, b):
    n = h.shape[0]
    tile_rows = ROW_TM // SUB
    row = pl.BlockSpec((tile_rows, SUB, D_MODEL), lambda i, pos: (i, 0, 0))
    vec = pl.BlockSpec((1, D_MODEL), lambda i, pos: (0, 0))
    grid_spec = pltpu.PrefetchScalarGridSpec(
        num_scalar_prefetch=1,
        grid=(n // ROW_TM,),
        in_specs=[row, pl.BlockSpec(memory_space=pl.ANY), vec, vec],
        out_specs=row,
        scratch_shapes=[pltpu.VMEM((2, tile_rows, SUB, D_MODEL), _F32),
                        pltpu.SemaphoreType.DMA((2,))],
    )
    out = pl.pallas_call(
        _combine_kernel,
        grid_spec=grid_spec,
        out_shape=jax.ShapeDtypeStruct((n // SUB, SUB, D_MODEL), _F32),
        compiler_params=_params("arbitrary"),
        name="combine",
    )(pos, _tiles(h), _tiles(y_sorted), g.reshape(1, D_MODEL), b.reshape(1, D_MODEL))
    return out.reshape(n, D_MODEL)


def _moe_layer(h, router_w, router_b, w_gate, w_up, w_down, ln_g, ln_b):
    cls, rank, cnt, gext = _router(h, router_w, router_b)
    counts = cnt[:N_CLASSES, 0].astype(jnp.int32)
    tiles = (counts + MOE_TM - 1) // MOE_TM
    tile_end = jnp.cumsum(tiles)
    class_start = (tile_end - tiles) * MOE_TM
    classes = jnp.arange(N_CLASSES, dtype=jnp.int32)
    pos = jnp.sum(jnp.where(cls[0][:, None] == classes[None, :], class_start[None, :], 0),
                  axis=1) + rank[0]
    n_used = tile_end[-1]
    tile_ids = jnp.minimum(jnp.arange(MOE_TILES, dtype=jnp.int32), n_used - 1)
    tile_cls = jnp.sum((tile_ids[:, None] >= tile_end[None, :]).astype(jnp.int32), axis=1)
    class_lo, class_hi = _class_experts()
    x_sorted = _dispatch(pos, h, gext)
    y_sorted = _experts(x_sorted, class_lo[tile_cls], class_hi[tile_cls],
                        n_used.reshape(1).astype(jnp.int32), w_gate, w_up, w_down)
    return _combine(pos, h, y_sorted, ln_g, ln_b)


def kernel(x, meta_tokens, mamba_w_in, mamba_conv_w, mamba_conv_b, mamba_dt_bias, mamba_a_log,
           mamba_d_skip, mamba_norm_w, mamba_w_out, sb_w_q, sb_w_o, shared_w_k, shared_w_v,
           ln_mix_g, ln_mix_b, ln_ffn_g, ln_ffn_b, router_w, router_b, moe_w_gate, moe_w_up,
           moe_w_down):
    bsz = x.shape[0]
    meta = jnp.broadcast_to(meta_tokens.astype(x.dtype)[None], (bsz, N_META, D_MODEL))
    h = jnp.concatenate([jnp.zeros((bsz, PAD, D_MODEL), x.dtype), meta, x], axis=1)
    h = h.reshape(NP, D_MODEL)
    kv = None
    for layer in range(DEPTH):
        if layer < N_A_LAYERS:
            w_in = mamba_w_in[layer].astype(_BF16)
            w_dt = jnp.pad(w_in[:, ZX_DIM:], ((0, 0), (0, LANES - SSM_HEADS)))
            proj = _in_proj(h, w_in, mamba_conv_w[layer], mamba_conv_b[layer])
            y = _ssd(proj.reshape(BATCH, LP, ZX_DIM), h.reshape(BATCH, LP, D_MODEL), w_dt,
                     mamba_dt_bias[layer], mamba_a_log[layer], mamba_d_skip[layer])
            h = _mamba_out(y.reshape(NP, D_INNER), proj, mamba_norm_w[layer],
                           mamba_w_out[layer].astype(_BF16), h, ln_mix_g[layer], ln_mix_b[layer])
        else:
            j = layer - N_A_LAYERS
            if kv is None:
                w_kv = jnp.concatenate([shared_w_k, shared_w_v], axis=1).astype(_BF16)
                kv = _matmul(h, w_kv, 2 * SB_WIDTH, 512, 2 * SB_WIDTH, _BF16).reshape(BATCH, LP, 2 * SB_WIDTH)
            q = _matmul(h, sb_w_q[j].astype(_BF16), SB_WIDTH, 512, 1024, _BF16,
                        scale=SB_SCALE * LOG2_E)
            o = _stick_breaking(q.reshape(BATCH, LP, SB_WIDTH), kv)
            h = _attn_out(o.reshape(NP, SB_WIDTH), sb_w_o[j].astype(_BF16), h,
                          ln_mix_g[layer], ln_mix_b[layer])
        h = _moe_layer(h, router_w, router_b, moe_w_gate[layer].astype(_BF16),
                       moe_w_up[layer].astype(_BF16), moe_w_down[layer].astype(_BF16),
                       ln_ffn_g[layer], ln_ffn_b[layer])
    return h.reshape(bsz, LP, D_MODEL)[:, PAD + N_META:]
```
